```python
import math
import jax, jax.numpy as jnp
from jax import lax
import numpy as np

D_MODEL = 1024
BATCH = 4
SEQ = 4096
DEPTH = 1
DEC_BATCH = 32
DEC_SEQ = 2048
PAST_LEN = 128

MIX_WIDTH = D_MODEL
ATTN_WIDTH = MIX_WIDTH // 2
FOURIER_WIDTH = MIX_WIDTH - ATTN_WIDTH
DIFF_HEAD_DIM = 64
N_DIFF_HEADS = ATTN_WIDTH // (2 * DIFF_HEAD_DIM)
N_FOURIER_GROUPS = 4
FOURIER_GROUP = FOURIER_WIDTH // N_FOURIER_GROUPS
IN_WIDTH = 3 * ATTN_WIDTH + FOURIER_WIDTH
Q_BLOCK = 128
N_EXPERTS = 32
TOP_K = 4
D_EXPERT = D_MODEL
SWIGLU_LIMIT = 7.0
SWIGLU_ALPHA = 1.702
MOE_BLOCK = 256
PLE_DIM = 256
EPS = 1e-6

kernel_name = "hybrid_diffattn_fnet_moe_encoder"


def rms_norm(x, g):
    xf = x.astype(jnp.float32)
    y = xf * lax.rsqrt(jnp.mean(xf * xf, axis=-1, keepdims=True) + EPS)
    return (y * g.astype(jnp.float32)).astype(x.dtype)


def diff_attention(q1, q2, k1, k2, v, lam):
    B, H, S, dh = q1.shape
    nq = S // Q_BLOCK
    slopes = jnp.exp2(-8.0 * jnp.arange(1, H + 1, dtype=jnp.float32) / H)
    kk = jnp.stack([k1, k2])
    qq = jnp.stack([q1, q2]).reshape(2, B, H, nq, Q_BLOCK, dh)
    qq = jnp.moveaxis(qq, 3, 0)
    qpos = jnp.arange(S, dtype=jnp.int32).reshape(nq, Q_BLOCK)
    kpos = jnp.arange(S, dtype=jnp.int32)
    scale = dh ** -0.5

    def block(args):
        qb, qp = args
        s = jnp.einsum('mbhqd,mbhkd->mbhqk', qb, kk, preferred_element_type=jnp.float32) * scale
        dist = jnp.abs(qp[:, None] - kpos[None, :]).astype(jnp.float32)
        s = s - slopes[:, None, None] * dist
        pr = jax.nn.softmax(s, axis=-1)
        a = (pr[0] - lam * pr[1]).astype(v.dtype)
        return jnp.einsum('bhqk,bhkd->bhqd', a, v)

    o = lax.map(block, (qq, qpos))
    return jnp.transpose(o, (1, 0, 3, 2, 4)).reshape(B, S, H, 2 * dh)


def fourier_mix(u):
    B, S, W = u.shape
    ug = u.reshape(B, S, N_FOURIER_GROUPS, FOURIER_GROUP).astype(jnp.float32)
    f = jnp.fft.fftn(ug, axes=(1, 3), norm='ortho').real
    return f.reshape(B, S, W).astype(u.dtype)


def moe(h, router_w, router_b, w_up, b_up, w_down, b_down):
    B, S, D = h.shape
    N = B * S
    NK = N * TOP_K
    hf = h.reshape(N, D)
    logits = jnp.matmul(hf, router_w, preferred_element_type=jnp.float32) + router_b.astype(jnp.float32)
    top_v, top_e = lax.top_k(logits, TOP_K)
    gates = jax.nn.softmax(top_v, axis=-1)
    flat_e = top_e.reshape(-1)
    flat_tok = jnp.arange(NK, dtype=jnp.int32) // TOP_K
    flat_g = gates.reshape(-1)
    order = jnp.argsort(flat_e, stable=True)
    sorted_e = flat_e[order]
    counts = jnp.bincount(flat_e, length=N_EXPERTS)
    starts = jnp.cumsum(counts) - counts
    padded = (counts + MOE_BLOCK - 1) // MOE_BLOCK * MOE_BLOCK
    pad_ends = jnp.cumsum(padded)
    pad_starts = pad_ends - padded
    dest = pad_starts[sorted_e] + jnp.arange(NK, dtype=jnp.int32) - starts[sorted_e]
    n_blocks = -(-NK // MOE_BLOCK) + N_EXPERTS
    P = n_blocks * MOE_BLOCK
    slot_tok = jnp.zeros((P,), jnp.int32).at[dest].set(flat_tok[order])
    slot_g = jnp.zeros((P,), jnp.float32).at[dest].set(flat_g[order])
    block_e = jnp.clip(jnp.searchsorted(pad_ends, jnp.arange(n_blocks) * MOE_BLOCK, side='right'),
                       0, N_EXPERTS - 1).astype(jnp.int32)

    def run(args):
        tok, e, g = args
        xb = hf[tok]
        u = xb @ w_up[e] + b_up[e]
        glu = jnp.minimum(u[:, :D_EXPERT], SWIGLU_LIMIT)
        lin = jnp.clip(u[:, D_EXPERT:], -SWIGLU_LIMIT, SWIGLU_LIMIT)
        a = glu * jax.nn.sigmoid(SWIGLU_ALPHA * glu) * (lin + 1.0)
        return (a @ w_down[e] + b_down[e]) * g[:, None].astype(h.dtype)

    out = lax.map(run, (slot_tok.reshape(n_blocks, MOE_BLOCK), block_e,
                        slot_g.reshape(n_blocks, MOE_BLOCK)))
    y = jnp.zeros((N, D), h.dtype).at[slot_tok].add(out.reshape(P, D))
    return y.reshape(B, S, D)


def trunk(x, p, norm_mix, w_in, lambda_q1, lambda_k1, lambda_q2, lambda_k2, subln_g,
          fourier_norm, w_out, norm_ffn, router_w, router_b, w_up, b_up, w_down, b_down,
          norm_ple, w_ple_gate, w_ple_proj, final_norm):
    B, S, _ = x.shape
    H, dh = N_DIFF_HEADS, DIFF_HEAD_DIM
    for i in range(DEPTH):
        h = rms_norm(x, norm_mix[i])
        proj = h @ w_in[i]
        q = proj[..., :ATTN_WIDTH].reshape(B, S, H, 2, dh)
        k = proj[..., ATTN_WIDTH:2 * ATTN_WIDTH].reshape(B, S, H, 2, dh)
        v = proj[..., 2 * ATTN_WIDTH:3 * ATTN_WIDTH].reshape(B, S, H, 2 * dh)
        u = proj[..., 3 * ATTN_WIDTH:]
        q1 = jnp.transpose(q[..., 0, :], (0, 2, 1, 3))
        q2 = jnp.transpose(q[..., 1, :], (0, 2, 1, 3))
        k1 = jnp.transpose(k[..., 0, :], (0, 2, 1, 3))
        k2 = jnp.transpose(k[..., 1, :], (0, 2, 1, 3))
        v = jnp.transpose(v, (0, 2, 1, 3))
        lam_init = 0.8 - 0.6 * math.exp(-0.3 * i)
        lam = (jnp.exp(jnp.sum(lambda_q1[i].astype(jnp.float32) * lambda_k1[i].astype(jnp.float32)))
               - jnp.exp(jnp.sum(lambda_q2[i].astype(jnp.float32) * lambda_k2[i].astype(jnp.float32)))
               + lam_init)
        o = diff_attention(q1, q2, k1, k2, v, lam)
        o = (rms_norm(o, subln_g[i]) * (1.0 - lam_init)).reshape(B, S, ATTN_WIDTH)
        f = rms_norm(fourier_mix(u), fourier_norm[i])
        x = x + jnp.concatenate([o, f], axis=-1) @ w_out[i]
        x = x + moe(rms_norm(x, norm_ffn[i]), router_w[i], router_b[i], w_up[i], b_up[i],
                    w_down[i], b_down[i])
        gate = jax.nn.sigmoid(rms_norm(x, norm_ple[i]) @ w_ple_gate[i])
        x = x + gate * (p[i] @ w_ple_proj[i])
    return rms_norm(x, final_norm)


def setup_inputs(seed: int = 0) -> dict:
    key = jax.random.key(seed)
    ks = jax.random.split(key, 24)
    f32 = jnp.float32
    nrm = lambda k, shape, s: jax.random.normal(k, shape, f32) * s
    gain = lambda k, shape: 1.0 + 0.05 * jax.random.normal(k, shape, f32)
    return {
        "x_prompt": nrm(ks[0], (BATCH, SEQ, D_MODEL), 1.0),
        "x_sample": nrm(ks[1], (DEC_BATCH, DEC_SEQ, D_MODEL), 1.0),
        "p_prompt": nrm(ks[2], (DEPTH, BATCH, SEQ, PLE_DIM), 1.0),
        "p_sample": nrm(ks[3], (DEPTH, DEC_BATCH, DEC_SEQ, PLE_DIM), 1.0),
        "norm_mix": gain(ks[4], (DEPTH, D_MODEL)),
        "w_in": nrm(ks[5], (DEPTH, D_MODEL, IN_WIDTH), D_MODEL ** -0.5),
        "lambda_q1": nrm(ks[6], (DEPTH, DIFF_HEAD_DIM), 0.1),
        "lambda_k1": nrm(ks[7], (DEPTH, DIFF_HEAD_DIM), 0.1),
        "lambda_q2": nrm(ks[8], (DEPTH, DIFF_HEAD_DIM), 0.1),
        "lambda_k2": nrm(ks[9], (DEPTH, DIFF_HEAD_DIM), 0.1),
        "subln_g": gain(ks[10], (DEPTH, 2 * DIFF_HEAD_DIM)),
        "fourier_norm": gain(ks[11], (DEPTH, FOURIER_WIDTH)),
        "w_out": nrm(ks[12], (DEPTH, MIX_WIDTH, D_MODEL), MIX_WIDTH ** -0.5),
        "norm_ffn": gain(ks[13], (DEPTH, D_MODEL)),
        "router_w": nrm(ks[14], (DEPTH, D_MODEL, N_EXPERTS), D_MODEL ** -0.5),
        "router_b": nrm(ks[15], (DEPTH, N_EXPERTS), 0.01),
        "w_up": nrm(ks[16], (DEPTH, N_EXPERTS, D_MODEL, 2 * D_EXPERT), D_MODEL ** -0.5),
        "b_up": nrm(ks[17], (DEPTH, N_EXPERTS, 2 * D_EXPERT), 0.01),
        "w_down": nrm(ks[18], (DEPTH, N_EXPERTS, D_EXPERT, D_MODEL), D_EXPERT ** -0.5),
        "b_down": nrm(ks[19], (DEPTH, N_EXPERTS, D_MODEL), 0.01),
        "norm_ple": gain(ks[20], (DEPTH, D_MODEL)),
        "w_ple_gate": nrm(ks[21], (DEPTH, D_MODEL, D_MODEL), D_MODEL ** -0.5),
        "w_ple_proj": nrm(ks[22], (DEPTH, PLE_DIM, D_MODEL), PLE_DIM ** -0.5),
        "final_norm": gain(ks[23], (D_MODEL,)),
    }


def reference(x_prompt, x_sample, p_prompt, p_sample, norm_mix, w_in, lambda_q1, lambda_k1,
              lambda_q2, lambda_k2, subln_g, fourier_norm, w_out, norm_ffn, router_w, router_b,
              w_up, b_up, w_down, b_down, norm_ple, w_ple_gate, w_ple_proj, final_norm):
    y_prompt = trunk(x_prompt, p_prompt, norm_mix, w_in, lambda_q1, lambda_k1, lambda_q2, lambda_k2,
                     subln_g, fourier_norm, w_out, norm_ffn, router_w, router_b, w_up, b_up,
                     w_down, b_down, norm_ple, w_ple_gate, w_ple_proj, final_norm)
    y_sample = trunk(x_sample, p_sample, norm_mix, w_in, lambda_q1, lambda_k1, lambda_q2, lambda_k2,
                     subln_g, fourier_norm, w_out, norm_ffn, router_w, router_b, w_up, b_up,
                     w_down, b_down, norm_ple, w_ple_gate, w_ple_proj, final_norm)
    return (y_prompt, y_sample)
```

```python
import functools
import math

import jax
import jax.numpy as jnp
from jax import lax
from jax.experimental import pallas as pl
from jax.experimental.pallas import tpu as pltpu

F32 = jnp.float32
BF16 = jnp.bfloat16
I32 = jnp.int32

D_MODEL = 1024
ATTN_WIDTH = 512
FOURIER_WIDTH = 512
HEAD_DIM = 64
N_HEADS = 4
N_GROUPS = 4
GROUP = 128
N_EXPERTS = 32
TOP_K = 4
D_EXPERT = 1024
SWIGLU_LIMIT = 7.0
SWIGLU_ALPHA = 1.702
PLE_DIM = 256
EPS = 1e-6
LAM_INIT = 0.8 - 0.6 * math.exp(-0.3 * 0)
TWIDDLE_ROWS = 64

VMEM_LIMIT = 56 * 1024 * 1024


def _params(n_axes):
    return pltpu.CompilerParams(dimension_semantics=("arbitrary",) * n_axes,
                                vmem_limit_bytes=VMEM_LIMIT)


def _rms(x, g):
    return x * lax.rsqrt(jnp.mean(x * x, axis=-1, keepdims=True) + EPS) * g


def _in_proj_kernel(nb_a, xa_ref, xb_ref, g_ref, w_ref, dft_ref, qkv_ref, ab_ref):
    i = pl.program_id(0)
    x = jnp.where(i < nb_a, xa_ref[...], xb_ref[...])
    h = _rms(x, g_ref[...]).astype(BF16)
    for c in range(3):
        pc = jnp.dot(h, w_ref[:, c * ATTN_WIDTH:(c + 1) * ATTN_WIDTH], preferred_element_type=F32)
        if c == 0:
            pc = pc * (HEAD_DIM ** -0.5)
        qkv_ref[:, c * ATTN_WIDTH:(c + 1) * ATTN_WIDTH] = pc.astype(BF16)
    u = jnp.dot(h, w_ref[:, 3 * ATTN_WIDTH:], preferred_element_type=F32).astype(BF16)
    for g in range(N_GROUPS):
        r = jnp.dot(u[:, g * GROUP:(g + 1) * GROUP], dft_ref[...], preferred_element_type=F32)
        ab_ref[:, g * GROUP:(g + 1) * GROUP] = r[:, :GROUP].astype(BF16)
        ab_ref[:, FOURIER_WIDTH + g * GROUP:FOURIER_WIDTH + (g + 1) * GROUP] = r[:, GROUP:].astype(BF16)


def _in_proj(xa, xb, g, w_bf, dft_bf, tm):
    na, nb = xa.shape[0], xb.shape[0]
    nba, nbb = na // tm, nb // tm
    n = na + nb
    return pl.pallas_call(
        functools.partial(_in_proj_kernel, nba),
        grid=(nba + nbb,),
        in_specs=[
            pl.BlockSpec((tm, D_MODEL), lambda i: (jnp.minimum(i, nba - 1), 0)),
            pl.BlockSpec((tm, D_MODEL), lambda i: (jnp.maximum(i - nba, 0), 0)),
            pl.BlockSpec((1, D_MODEL), lambda i: (0, 0)),
            pl.BlockSpec((D_MODEL, 4 * ATTN_WIDTH), lambda i: (0, 0)),
            pl.BlockSpec((GROUP, 2 * GROUP), lambda i: (0, 0)),
        ],
        out_specs=[
            pl.BlockSpec((tm, 3 * ATTN_WIDTH), lambda i: (i, 0)),
            pl.BlockSpec((tm, 2 * FOURIER_WIDTH), lambda i: (i, 0)),
        ],
        out_shape=[jax.ShapeDtypeStruct((n, 3 * ATTN_WIDTH), BF16),
                   jax.ShapeDtypeStruct((n, 2 * FOURIER_WIDTH), BF16)],
        compiler_params=_params(1),
        name="in_proj",
    )(xa, xb, g, w_bf, dft_bf)


def _attn_kernel(seq, tq, tk, lq1_ref, lk1_ref, lq2_ref, lk2_ref, g_ref, q_ref, k_ref, v_ref,
                 o_ref, m_sc, l_sc, acc_sc):
    h = pl.program_id(1)
    qi = pl.program_id(2)
    slope = jnp.where(h == 0, 0.25, jnp.where(h == 1, 0.0625, jnp.where(h == 2, 0.015625, 0.00390625)))
    slope = slope.astype(F32)
    lam = (jnp.exp(jnp.sum(lq1_ref[...] * lk1_ref[...], axis=-1, keepdims=True))
           - jnp.exp(jnp.sum(lq2_ref[...] * lk2_ref[...], axis=-1, keepdims=True)) + LAM_INIT)

    q = q_ref[...]
    lane = lax.broadcasted_iota(I32, q.shape, 1)
    zero = jnp.zeros_like(q)
    qz = (jnp.where(lane < HEAD_DIM, q, zero), jnp.where(lane >= HEAD_DIM, q, zero))
    rowpos = qi * tq + lax.broadcasted_iota(I32, (tq, tk), 0)
    coliota = lax.broadcasted_iota(I32, (tq, tk), 1)

    m_sc[...] = jnp.full(m_sc.shape, -jnp.inf, F32)
    l_sc[...] = jnp.zeros(l_sc.shape, F32)
    acc_sc[...] = jnp.zeros(acc_sc.shape, F32)

    def body(j, carry):
        start = pl.multiple_of(j * tk, tk)
        kc = k_ref[pl.ds(start, tk), :]
        vc = v_ref[pl.ds(start, tk), :]
        bias = slope * jnp.abs(rowpos - (coliota + j * tk)).astype(F32)
        for m in range(2):
            s = lax.dot_general(qz[m], kc, (((1,), (1,)), ((), ())), preferred_element_type=F32) - bias
            m_old = m_sc[m]
            m_new = jnp.maximum(m_old, jnp.max(s, axis=1, keepdims=True))
            p = jnp.exp(s - m_new)
            alpha = jnp.exp(m_old - m_new)
            l_sc[m] = alpha * l_sc[m] + jnp.sum(p, axis=1, keepdims=True)
            acc_sc[m] = alpha * acc_sc[m] + jnp.dot(p.astype(BF16), vc, preferred_element_type=F32)
            m_sc[m] = m_new
        return carry

    lax.fori_loop(0, seq // tk, body, 0)
    o = acc_sc[0] / l_sc[0] - lam * (acc_sc[1] / l_sc[1])
    o_ref[...] = (_rms(o, g_ref[...]) * (1.0 - LAM_INIT)).astype(BF16)


def _attention(qkv, row_off, batch, seq, lams, subln_g, tq, tk):
    nq = seq // tq
    qb0 = row_off // tq
    kb0 = row_off // seq
    vec = pl.BlockSpec((1, HEAD_DIM), lambda b, h, i: (0, 0))
    return pl.pallas_call(
        functools.partial(_attn_kernel, seq, tq, tk),
        grid=(batch, N_HEADS, nq),
        in_specs=[vec, vec, vec, vec,
                  pl.BlockSpec((1, 2 * HEAD_DIM), lambda b, h, i: (0, 0)),
                  pl.BlockSpec((tq, 2 * HEAD_DIM), lambda b, h, i: (qb0 + b * nq + i, h)),
                  pl.BlockSpec((seq, 2 * HEAD_DIM), lambda b, h, i: (kb0 + b, N_HEADS + h)),
                  pl.BlockSpec((seq, 2 * HEAD_DIM), lambda b, h, i: (kb0 + b, 2 * N_HEADS + h))],
        out_specs=pl.BlockSpec((tq, 2 * HEAD_DIM), lambda b, h, i: (b * nq + i, h)),
        out_shape=jax.ShapeDtypeStruct((batch * seq, ATTN_WIDTH), BF16),
        scratch_shapes=[pltpu.VMEM((2, tq, 1), F32), pltpu.VMEM((2, tq, 1), F32),
                        pltpu.VMEM((2, tq, 2 * HEAD_DIM), F32)],
        compiler_params=_params(3),
        name="attention",
    )(*lams, subln_g, qkv, qkv, qkv)


def _seq_dft_kernel(seq, tm, t1c_ref, t1s_ref, t2c_ref, t2s_ref, g_ref, ab_ref, f_ref, lc_sc, ls_sc):
    b = pl.program_id(1)

    @pl.when(b == 0)
    def _():
        t2c = t2c_ref[...]
        t2s = t2s_ref[...]
        for a in range(tm // TWIDDLE_ROWS):
            c1 = t1c_ref[a:a + 1, :]
            s1 = t1s_ref[a:a + 1, :]
            rows = slice(a * TWIDDLE_ROWS, (a + 1) * TWIDDLE_ROWS)
            lc_sc[rows, :] = (c1 * t2c - s1 * t2s).astype(BF16)
            ls_sc[rows, :] = (-(s1 * t2c + c1 * t2s)).astype(BF16)

    f = (jnp.dot(lc_sc[...], ab_ref[:, :FOURIER_WIDTH], preferred_element_type=F32)
         + jnp.dot(ls_sc[...], ab_ref[:, FOURIER_WIDTH:], preferred_element_type=F32))
    f = f * (1.0 / math.sqrt(seq * GROUP))
    f_ref[...] = _rms(f, g_ref[...]).astype(BF16)


def _twiddle_tables(seq):
    k = jnp.arange(seq, dtype=I32)[None, :]
    a = (jnp.arange(seq // TWIDDLE_ROWS, dtype=I32) * TWIDDLE_ROWS)[:, None]
    b = jnp.arange(TWIDDLE_ROWS, dtype=I32)[:, None]
    ang1 = ((a * k) % seq).astype(F32) * (2.0 * math.pi / seq)
    ang2 = ((b * k) % seq).astype(F32) * (2.0 * math.pi / seq)
    return jnp.cos(ang1), jnp.sin(ang1), jnp.cos(ang2), jnp.sin(ang2)


def _seq_dft(ab, row_off, batch, seq, fourier_g, tm):
    t1c, t1s, t2c, t2s = _twiddle_tables(seq)
    nr = seq // tm
    na = tm // TWIDDLE_ROWS
    kb0 = row_off // seq
    t1 = pl.BlockSpec((na, seq), lambda i, b: (i, 0))
    t2 = pl.BlockSpec((TWIDDLE_ROWS, seq), lambda i, b: (0, 0))
    return pl.pallas_call(
        functools.partial(_seq_dft_kernel, seq, tm),
        grid=(nr, batch),
        in_specs=[t1, t1, t2, t2,
                  pl.BlockSpec((1, FOURIER_WIDTH), lambda i, b: (0, 0)),
                  pl.BlockSpec((seq, 2 * FOURIER_WIDTH), lambda i, b: (kb0 + b, 0))],
        out_specs=pl.BlockSpec((tm, FOURIER_WIDTH), lambda i, b: (b * nr + i, 0)),
        out_shape=jax.ShapeDtypeStruct((batch * seq, FOURIER_WIDTH), BF16),
        scratch_shapes=[pltpu.VMEM((tm, seq), BF16), pltpu.VMEM((tm, seq), BF16)],
        compiler_params=_params(2),
        name="seq_dft",
    )(t1c, t1s, t2c, t2s, fourier_g, ab)


def _out_route_kernel(nb_a, tm, xa_ref, xb_ref, oa_ref, ob_ref, fa_ref, fb_ref, wo_ref, g_ref,
                      rw_ref, rb_ref, x1_ref, h2_ref, e_ref, r_ref, gt_ref, cnt_ref, carry_sc):
    i = pl.program_id(0)
    first = i < nb_a

    @pl.when(i == 0)
    def _():
        carry_sc[...] = jnp.zeros(carry_sc.shape, F32)

    x = jnp.where(first, xa_ref[...], xb_ref[...])
    o = jnp.where(first, oa_ref[...], ob_ref[...])
    f = jnp.where(first, fa_ref[...], fb_ref[...])
    x1 = (x + jnp.dot(o, wo_ref[:ATTN_WIDTH, :], preferred_element_type=F32)
          + jnp.dot(f, wo_ref[ATTN_WIDTH:, :], preferred_element_type=F32))
    x1_ref[...] = x1
    h2 = _rms(x1, g_ref[...])
    h2_ref[...] = h2

    lg = lax.dot_general(rw_ref[...], h2.astype(BF16), (((1,), (1,)), ((), ())),
                         preferred_element_type=F32) + rb_ref[...]
    eidx = lax.broadcasted_iota(I32, lg.shape, 0)
    work = lg
    vals, hots = [], []
    for k in range(TOP_K):
        mx = jnp.max(work, axis=0, keepdims=True)
        sel = jnp.min(jnp.where(work == mx, eidx, N_EXPERTS), axis=0, keepdims=True)
        hot = eidx == sel
        work = jnp.where(hot, -jnp.inf, work)
        vals.append(mx)
        hots.append(hot)
        e_ref[0, k:k + 1, :] = sel
    ex = [jnp.exp(v - vals[0]) for v in vals]
    den = ex[0] + ex[1] + ex[2] + ex[3]
    for k in range(TOP_K):
        gt_ref[0, k:k + 1, :] = ex[k] / den

    multi = jnp.zeros(lg.shape, F32)
    for k in range(TOP_K):
        multi = multi + hots[k].astype(F32)
    tri = (lax.broadcasted_iota(I32, (tm, tm), 0) < lax.broadcasted_iota(I32, (tm, tm), 1)).astype(BF16)
    base = jnp.dot(multi.astype(BF16), tri, preferred_element_type=F32) + carry_sc[...]
    for k in range(TOP_K):
        r_ref[0, k:k + 1, :] = jnp.sum(jnp.where(hots[k], base, 0.0), axis=0, keepdims=True).astype(I32)
    carry_sc[...] = carry_sc[...] + jnp.sum(multi, axis=1, keepdims=True)
    cnt_ref[...] = carry_sc[...].astype(I32)


def _out_route(xa, xb, oa, ob, fa, fb, wo_bf, g, rw_t_bf, rb_col, tm):
    na, nb = xa.shape[0], xb.shape[0]
    nba, nbb = na // tm, nb // tm
    n, nblk = na + nb, nba + nbb

    def sel_a(w):
        return pl.BlockSpec((tm, w), lambda i: (jnp.minimum(i, nba - 1), 0))

    def sel_b(w):
        return pl.BlockSpec((tm, w), lambda i: (jnp.maximum(i - nba, 0), 0))

    slot = pl.BlockSpec((1, TOP_K, tm), lambda i: (i, 0, 0))
    return pl.pallas_call(
        functools.partial(_out_route_kernel, nba, tm),
        grid=(nblk,),
        in_specs=[sel_a(D_MODEL), sel_b(D_MODEL), sel_a(ATTN_WIDTH), sel_b(ATTN_WIDTH),
                  sel_a(FOURIER_WIDTH), sel_b(FOURIER_WIDTH),
                  pl.BlockSpec((D_MODEL, D_MODEL), lambda i: (0, 0)),
                  pl.BlockSpec((1, D_MODEL), lambda i: (0, 0)),
                  pl.BlockSpec((N_EXPERTS, D_MODEL), lambda i: (0, 0)),
                  pl.BlockSpec((N_EXPERTS, 1), lambda i: (0, 0))],
        out_specs=[pl.BlockSpec((tm, D_MODEL), lambda i: (i, 0)),
                   pl.BlockSpec((tm, D_MODEL), lambda i: (i, 0)),
                   slot, slot, slot,
                   pl.BlockSpec((N_EXPERTS, 1), lambda i: (0, 0))],
        out_shape=[jax.ShapeDtypeStruct((n, D_MODEL), F32),
                   jax.ShapeDtypeStruct((n, D_MODEL), F32),
                   jax.ShapeDtypeStruct((nblk, TOP_K, tm), I32),
                   jax.ShapeDtypeStruct((nblk, TOP_K, tm), I32),
                   jax.ShapeDtypeStruct((nblk, TOP_K, tm), F32),
                   jax.ShapeDtypeStruct((N_EXPERTS, 1), I32)],
        scratch_shapes=[pltpu.VMEM((N_EXPERTS, 1), F32)],
        compiler_params=_params(1),
        name="out_route",
    )(xa, xb, oa, ob, fa, fb, wo_bf, g, rw_t_bf, rb_col)


def _dest_kernel(ps_ref, e_ref, r_ref, d_ref):
    e = e_ref[...]
    acc = jnp.zeros(e.shape, I32)
    for x in range(N_EXPERTS):
        acc = jnp.where(e == x, ps_ref[x], acc)
    d_ref[...] = acc + r_ref[...]


def _dest(pad_starts, top_e, rank):
    nblk, _, tm = top_e.shape
    slot = pl.BlockSpec((1, TOP_K, tm), lambda i: (i, 0, 0))
    return pl.pallas_call(
        _dest_kernel,
        grid=(nblk,),
        in_specs=[pl.BlockSpec(memory_space=pltpu.SMEM), slot, slot],
        out_specs=slot,
        out_shape=jax.ShapeDtypeStruct(top_e.shape, I32),
        compiler_params=_params(1),
        name="dest",
    )(pad_starts, top_e, rank)


def _dispatch_kernel(tm, tme, n_zero, zb_ref, d_ref, h_ref, xs_ref, zeros_sc, sem, zsem):
    i = pl.program_id(0)

    @pl.when(i == 0)
    def _():
        zeros_sc[...] = jnp.zeros(zeros_sc.shape, F32)

        def zero_copy(j):
            return pltpu.make_async_copy(zeros_sc, xs_ref.at[pl.ds(zb_ref[j] * tme, tme)], zsem)

        for j in range(n_zero):
            pl.when(zb_ref[j] >= 0)(lambda j=j: zero_copy(j).start())
        for j in range(n_zero):
            pl.when(zb_ref[j] >= 0)(lambda j=j: zero_copy(j).wait())

    def body(t, carry):
        for k in range(TOP_K):
            pltpu.make_async_copy(h_ref.at[pl.ds(t, 1)], xs_ref.at[pl.ds(d_ref[0, k, t], 1)], sem).start()
        return carry

    lax.fori_loop(0, tm, body, 0)
    for k in range(TOP_K):
        pltpu.make_async_copy(h_ref, xs_ref.at[pl.ds(0, tm)], sem).wait()


def _dispatch(zero_blocks, dest, h2, n_rows, tme):
    nblk, _, tm = dest.shape
    n_zero = zero_blocks.shape[0]
    return pl.pallas_call(
        functools.partial(_dispatch_kernel, tm, tme, n_zero),
        grid_spec=pltpu.PrefetchScalarGridSpec(
            num_scalar_prefetch=1,
            grid=(nblk,),
            in_specs=[pl.BlockSpec((1, TOP_K, tm), lambda i, zb: (i, 0, 0), memory_space=pltpu.SMEM),
                      pl.BlockSpec((tm, D_MODEL), lambda i, zb: (i, 0))],
            out_specs=pl.BlockSpec(memory_space=pl.ANY),
            scratch_shapes=[pltpu.VMEM((tme, D_MODEL), F32), pltpu.SemaphoreType.DMA(()),
                            pltpu.SemaphoreType.DMA(())]),
        out_shape=jax.ShapeDtypeStruct((n_rows, D_MODEL), F32),
        compiler_params=_params(1),
        name="dispatch",
    )(zero_blocks, dest, h2)


def _experts_kernel(be_ref, nu_ref, xs_ref, wu_ref, bu_ref, wd_ref, bd_ref, ys_ref, a_sc):
    i = pl.program_id(0)

    @pl.when(i < nu_ref[0])
    def _():
        x = xs_ref[...].astype(BF16)
        cw = 256
        for c in range(D_EXPERT // cw):
            glu = jnp.dot(x, wu_ref[0, :, c * cw:(c + 1) * cw], preferred_element_type=F32) \
                + bu_ref[0, :, c * cw:(c + 1) * cw]
            lin = jnp.dot(x, wu_ref[0, :, D_EXPERT + c * cw:D_EXPERT + (c + 1) * cw],
                          preferred_element_type=F32) + bu_ref[0, :, D_EXPERT + c * cw:D_EXPERT + (c + 1) * cw]
            glu = jnp.minimum(glu, SWIGLU_LIMIT)
            lin = jnp.clip(lin, -SWIGLU_LIMIT, SWIGLU_LIMIT)
            a = glu * jax.nn.sigmoid(SWIGLU_ALPHA * glu) * (lin + 1.0)
            a_sc[:, c * cw:(c + 1) * cw] = a.astype(BF16)
        ys_ref[...] = jnp.dot(a_sc[...], wd_ref[0], preferred_element_type=F32) + bd_ref[0]

    @pl.when(i >= nu_ref[0])
    def _():
        ys_ref[...] = jnp.zeros(ys_ref.shape, F32)


def _experts(block_e, n_used, xs, wu_bf, bu, wd_bf, bd, tme):
    n_rows = xs.shape[0]
    nb = n_rows // tme

    def row_map(i, be, nu):
        return (jnp.minimum(i, nu[0] - 1), 0)

    def out_map(i, be, nu):
        return (i, 0)

    def exp_map(i, be, nu):
        return (be[i], 0, 0)

    return pl.pallas_call(
        _experts_kernel,
        grid_spec=pltpu.PrefetchScalarGridSpec(
            num_scalar_prefetch=2,
            grid=(nb,),
            in_specs=[pl.BlockSpec((tme, D_MODEL), row_map),
                      pl.BlockSpec((1, D_MODEL, 2 * D_EXPERT), exp_map),
                      pl.BlockSpec((1, 1, 2 * D_EXPERT), exp_map),
                      pl.BlockSpec((1, D_EXPERT, D_MODEL), exp_map),
                      pl.BlockSpec((1, 1, D_MODEL), exp_map)],
            out_specs=pl.BlockSpec((tme, D_MODEL), out_map),
            scratch_shapes=[pltpu.VMEM((tme, D_EXPERT), BF16)]),
        out_shape=jax.ShapeDtypeStruct((n_rows, D_MODEL), F32),
        compiler_params=_params(1),
        name="experts",
    )(block_e, n_used, xs, wu_bf, bu, wd_bf, bd)


def _final_kernel(tm, d_ref, x1_ref, gt_ref, p_ref, gp_ref, wg_ref, wp_ref, gf_ref, ys_ref, y_ref,
                  buf, sem):
    def body(t, carry):
        for k in range(TOP_K):
            pltpu.make_async_copy(ys_ref.at[pl.ds(d_ref[0, k, t], 1)], buf.at[k, pl.ds(t, 1)], sem).start()
        return carry

    lax.fori_loop(0, tm, body, 0)
    for k in range(TOP_K):
        pltpu.make_async_copy(ys_ref.at[pl.ds(0, tm)], buf.at[k], sem).wait()

    x2 = x1_ref[...]
    for k in range(TOP_K):
        x2 = x2 + buf[k] * gt_ref[:, k:k + 1]
    gate = jax.nn.sigmoid(jnp.dot(_rms(x2, gp_ref[...]).astype(BF16), wg_ref[...],
                                  preferred_element_type=F32))
    x3 = x2 + gate * jnp.dot(p_ref[...].astype(BF16), wp_ref[...], preferred_element_type=F32)
    y_ref[...] = _rms(x3, gf_ref[...])


def _final(dest, x1, gates_tok, p, g_ple, wg_bf, wp_bf, g_fin, ys, row_off):
    tm = dest.shape[2]
    n_tok = p.shape[0]
    b0 = row_off // tm
    return pl.pallas_call(
        functools.partial(_final_kernel, tm),
        grid=(n_tok // tm,),
        in_specs=[pl.BlockSpec((1, TOP_K, tm), lambda i: (b0 + i, 0, 0), memory_space=pltpu.SMEM),
                  pl.BlockSpec((tm, D_MODEL), lambda i: (b0 + i, 0)),
                  pl.BlockSpec((tm, TOP_K), lambda i: (b0 + i, 0)),
                  pl.BlockSpec((tm, PLE_DIM), lambda i: (i, 0)),
                  pl.BlockSpec((1, D_MODEL), lambda i: (0, 0)),
                  pl.BlockSpec((D_MODEL, D_MODEL), lambda i: (0, 0)),
                  pl.BlockSpec((PLE_DIM, D_MODEL), lambda i: (0, 0)),
                  pl.BlockSpec((1, D_MODEL), lambda i: (0, 0)),
                  pl.BlockSpec(memory_space=pl.ANY)],
        out_specs=pl.BlockSpec((tm, D_MODEL), lambda i: (i, 0)),
        out_shape=jax.ShapeDtypeStruct((n_tok, D_MODEL), F32),
        scratch_shapes=[pltpu.VMEM((TOP_K, tm, D_MODEL), F32), pltpu.SemaphoreType.DMA(())],
        compiler_params=_params(1),
        name="final",
    )(dest, x1, gates_tok, p, g_ple, wg_bf, wp_bf, g_fin, ys)


def _tiles(seq_a, seq_b):
    smin = min(seq_a, seq_b)
    return dict(tm=min(512, smin), tq=min(512, smin), tk=min(512, smin), tf=min(512, smin),
                tr=min(512, smin), tme=min(512, smin))


def _forward(x_a, x_b, p_a, p_b, norm_mix, w_in, lambda_q1, lambda_k1, lambda_q2, lambda_k2, subln_g,
             fourier_norm, w_out, norm_ffn, router_w, router_b, w_up, b_up, w_down, b_down, norm_ple,
             w_ple_gate, w_ple_proj, final_norm):
    (ba, sa, _), (bb, sb, _) = x_a.shape, x_b.shape
    na, nb = ba * sa, bb * sb
    n = na + nb
    t = _tiles(sa, sb)
    xa2, xb2 = x_a.reshape(na, D_MODEL), x_b.reshape(nb, D_MODEL)

    c = jnp.arange(GROUP, dtype=I32)
    ang = ((c[:, None] * c[None, :]) % GROUP).astype(F32) * (2.0 * math.pi / GROUP)
    dft = jnp.concatenate([jnp.cos(ang), jnp.sin(ang)], axis=1).astype(BF16)
    row = lambda v: v.reshape(1, -1).astype(F32)

    qkv, ab = _in_proj(xa2, xb2, row(norm_mix[0]), w_in[0].astype(BF16), dft, t["tm"])

    lams = (row(lambda_q1[0]), row(lambda_k1[0]), row(lambda_q2[0]), row(lambda_k2[0]))
    o_a = _attention(qkv, 0, ba, sa, lams, row(subln_g[0]), t["tq"], t["tk"])
    o_b = _attention(qkv, na, bb, sb, lams, row(subln_g[0]), t["tq"], t["tk"])
    f_a = _seq_dft(ab, 0, ba, sa, row(fourier_norm[0]), t["tf"])
    f_b = _seq_dft(ab, na, bb, sb, row(fourier_norm[0]), t["tf"])

    tr = t["tr"]
    x1, h2, top_e, rank, gates, counts = _out_route(
        xa2, xb2, o_a, o_b, f_a, f_b, w_out[0].astype(BF16), row(norm_ffn[0]),
        router_w[0].T.astype(BF16), router_b[0].reshape(N_EXPERTS, 1).astype(F32), tr)

    tme = t["tme"]
    counts = counts.reshape(N_EXPERTS)
    blocks = (counts + tme - 1) // tme
    blk_ends = jnp.cumsum(blocks)
    pad_starts = ((blk_ends - blocks) * tme).astype(I32)
    n_blocks = (n * TOP_K) // tme + N_EXPERTS
    n_used = blk_ends[-1:].astype(I32)
    bidx = jnp.minimum(jnp.arange(n_blocks, dtype=I32), n_used[0] - 1)
    block_e = jnp.sum((bidx[:, None] >= blk_ends[None, :]).astype(I32), axis=1).astype(I32)

    tail = n_used[0] + jnp.arange(N_EXPERTS, dtype=I32)
    zero_blocks = jnp.concatenate([jnp.where(blocks > 0, blk_ends - 1, -1).astype(I32),
                                   jnp.where(tail < n_blocks, tail, -1)])

    dest = _dest(pad_starts, top_e, rank)
    xs = _dispatch(zero_blocks, dest, h2, n_blocks * tme, tme)
    ys = _experts(block_e, n_used, xs, w_up[0].astype(BF16), b_up[0].reshape(N_EXPERTS, 1, -1),
                  w_down[0].astype(BF16), b_down[0].reshape(N_EXPERTS, 1, -1), tme)

    gates_tok = jnp.transpose(gates, (0, 2, 1)).reshape(n, TOP_K)
    fin = functools.partial(_final, dest, x1, gates_tok)
    wg, wp = w_ple_gate[0].astype(BF16), w_ple_proj[0].astype(BF16)
    y_a = fin(p_a[0].reshape(na, PLE_DIM), row(norm_ple[0]), wg, wp, row(final_norm), ys, 0)
    y_b = fin(p_b[0].reshape(nb, PLE_DIM), row(norm_ple[0]), wg, wp, row(final_norm), ys, na)
    return y_a.reshape(ba, sa, D_MODEL), y_b.reshape(bb, sb, D_MODEL)


def kernel(x_prompt, x_sample, p_prompt, p_sample, norm_mix, w_in, lambda_q1, lambda_k1, lambda_q2, lambda_k2, subln_g, fourier_norm, w_out, norm_ffn, router_w, router_b, w_up, b_up, w_down, b_down, norm_ple, w_ple_gate, w_ple_proj, final_norm):
    return _forward(x_prompt, x_sample, p_prompt, p_sample, norm_mix, w_in, lambda_q1, lambda_k1,
                    lambda_q2, lambda_k2, subln_g, fourier_norm, w_out, norm_ffn, router_w, router_b,
                    w_up, b_up, w_down, b_down, norm_ple, w_ple_gate, w_ple_proj, final_norm)
```

```python
import functools
import math

import jax
import jax.numpy as jnp
from jax import lax
from jax.experimental import pallas as pl
from jax.experimental.pallas import tpu as pltpu

F32 = jnp.float32
BF16 = jnp.bfloat16
I32 = jnp.int32

D_MODEL = 1024
ATTN_WIDTH = 512
FOURIER_WIDTH = 512
HEAD_DIM = 64
N_HEADS = 4
N_GROUPS = 4
GROUP = 128
N_EXPERTS = 32
TOP_K = 4
D_EXPERT = 1024
SWIGLU_LIMIT = 7.0
SWIGLU_ALPHA = 1.702
PLE_DIM = 256
EPS = 1e-6
LAM_INIT = 0.8 - 0.6 * math.exp(-0.3 * 0)
TWIDDLE_ROWS = 64

VMEM_LIMIT = 56 * 1024 * 1024


def _params(n_axes):
    return pltpu.CompilerParams(dimension_semantics=("arbitrary",) * n_axes,
                                vmem_limit_bytes=VMEM_LIMIT)


def _rms(x, g):
    return x * lax.rsqrt(jnp.mean(x * x, axis=-1, keepdims=True) + EPS) * g


def _in_proj_kernel(nb_a, xa_ref, xb_ref, g_ref, w_ref, dft_ref, qkv_ref, ab_ref):
    i = pl.program_id(0)
    x = jnp.where(i < nb_a, xa_ref[...], xb_ref[...])
    h = _rms(x, g_ref[...]).astype(BF16)
    for c in range(3):
        pc = jnp.dot(h, w_ref[:, c * ATTN_WIDTH:(c + 1) * ATTN_WIDTH], preferred_element_type=F32)
        if c == 0:
            pc = pc * (HEAD_DIM ** -0.5)
        qkv_ref[:, c * ATTN_WIDTH:(c + 1) * ATTN_WIDTH] = pc.astype(BF16)
    u = jnp.dot(h, w_ref[:, 3 * ATTN_WIDTH:], preferred_element_type=F32).astype(BF16)
    for g in range(N_GROUPS):
        r = jnp.dot(u[:, g * GROUP:(g + 1) * GROUP], dft_ref[...], preferred_element_type=F32)
        ab_ref[:, g * GROUP:(g + 1) * GROUP] = r[:, :GROUP].astype(BF16)
        ab_ref[:, FOURIER_WIDTH + g * GROUP:FOURIER_WIDTH + (g + 1) * GROUP] = r[:, GROUP:].astype(BF16)


def _in_proj(xa, xb, g, w_bf, dft_bf, tm):
    na, nb = xa.shape[0], xb.shape[0]
    nba, nbb = na // tm, nb // tm
    n = na + nb
    return pl.pallas_call(
        functools.partial(_in_proj_kernel, nba),
        grid=(nba + nbb,),
        in_specs=[
            pl.BlockSpec((tm, D_MODEL), lambda i: (jnp.minimum(i, nba - 1), 0)),
            pl.BlockSpec((tm, D_MODEL), lambda i: (jnp.maximum(i - nba, 0), 0)),
            pl.BlockSpec((1, D_MODEL), lambda i: (0, 0)),
            pl.BlockSpec((D_MODEL, 4 * ATTN_WIDTH), lambda i: (0, 0)),
            pl.BlockSpec((GROUP, 2 * GROUP), lambda i: (0, 0)),
        ],
        out_specs=[
            pl.BlockSpec((tm, 3 * ATTN_WIDTH), lambda i: (i, 0)),
            pl.BlockSpec((tm, 2 * FOURIER_WIDTH), lambda i: (i, 0)),
        ],
        out_shape=[jax.ShapeDtypeStruct((n, 3 * ATTN_WIDTH), BF16),
                   jax.ShapeDtypeStruct((n, 2 * FOURIER_WIDTH), BF16)],
        compiler_params=_params(1),
        name="in_proj",
    )(xa, xb, g, w_bf, dft_bf)


def _alibi_lanes(m, lane):
    data = (lane < HEAD_DIM) if m == 0 else (lane >= HEAD_DIM)
    return data, (HEAD_DIM if m == 0 else 0)


def _attn_kernel(seq, tq, lq1_ref, lk1_ref, lq2_ref, lk2_ref, g_ref, q_ref, k_ref, v_ref, o_ref,
                 kaug_sc, dbias_sc, qa_sc, m_sc0, l_sc0, a_sc0, m_sc1, l_sc1, a_sc1):
    h = pl.program_id(1)
    qi = pl.program_id(2)
    nk = seq // tq
    slope = jnp.where(h == 0, 0.25, jnp.where(h == 1, 0.0625, jnp.where(h == 2, 0.015625, 0.00390625)))
    slope = slope.astype(F32)
    m_sc, l_sc, a_sc = (m_sc0, m_sc1), (l_sc0, l_sc1), (a_sc0, a_sc1)
    nt = (((1,), (1,)), ((), ()))
    lane = lax.broadcasted_iota(I32, (tq, 2 * HEAD_DIM), 1)
    rows = lax.broadcasted_iota(I32, (tq, 2 * HEAD_DIM), 0)

    def features(fb, f0, f1, f2, f3):
        return jnp.where(lane == fb, f0, jnp.where(lane == fb + 1, f1, jnp.where(
            lane == fb + 2, f2, jnp.where(lane == fb + 3, f3, 0.0))))

    @pl.when(qi == 0)
    def _():
        r = lax.broadcasted_iota(I32, (tq, tq), 0)
        c = lax.broadcasted_iota(I32, (tq, tq), 1)
        dbias_sc[...] = -slope * jnp.abs(r - c).astype(F32)

        def build(j, carry):
            start = pl.multiple_of(j * tq, tq)
            kc = k_ref[pl.ds(start, tq), :]
            pos = start + rows
            jh = (pos >> 6).astype(F32)
            jl = (pos & 63).astype(F32)
            for m in range(2):
                data, fb = _alibi_lanes(m, lane)
                left = features(fb, -64.0 * slope, -slope, 64.0 * slope * jh, slope * jl)
                kaug_sc[m, 0, pl.ds(start, tq), :] = jnp.where(data, kc, left.astype(BF16))
                kaug_sc[m, 1, pl.ds(start, tq), :] = jnp.where(data, kc, (-left).astype(BF16))
            return carry

        lax.fori_loop(0, nk, build, 0)

    q = q_ref[...]
    ipos = qi * tq + rows
    q_plain = []
    for m in range(2):
        data, fb = _alibi_lanes(m, lane)
        feat = features(fb, (ipos >> 6).astype(F32), (ipos & 63).astype(F32), 1.0, 1.0)
        qa_sc[m] = jnp.where(data, q, feat.astype(BF16))
        q_plain.append(jnp.where(data, q, jnp.zeros_like(q)))

    def softmax_block(s, m_new):
        parts = [jnp.exp(s[:, g * 128:(g + 1) * 128] - m_new) for g in range(tq // 128)]
        row_part = parts[0]
        for p in parts[1:]:
            row_part = row_part + p
        return jnp.concatenate(parts, axis=1).astype(BF16), row_part

    dstart = pl.multiple_of(qi * tq, tq)
    vd = v_ref[pl.ds(dstart, tq), :]
    for m in range(2):
        s = lax.dot_general(q_plain[m], kaug_sc[m, 0, pl.ds(dstart, tq), :], nt,
                            preferred_element_type=F32) + dbias_sc[...]
        m_new = jnp.broadcast_to(jnp.max(s, axis=1, keepdims=True), (tq, 128))
        p, row_part = softmax_block(s, m_new)
        m_sc[m][...] = m_new
        l_sc[m][...] = row_part
        a_sc[m][...] = jnp.dot(p, vd, preferred_element_type=F32)

    def body(jj, carry):
        j = jj + (jj >= qi).astype(I32)
        side = (j > qi).astype(I32)
        start = pl.multiple_of(j * tq, tq)
        vc = v_ref[pl.ds(start, tq), :]
        for m in range(2):
            s = lax.dot_general(qa_sc[m], kaug_sc[m, side, pl.ds(start, tq), :], nt,
                                preferred_element_type=F32)
            m_old = m_sc[m][...]
            m_new = jnp.maximum(m_old, jnp.max(s, axis=1, keepdims=True))
            alpha = jnp.exp(m_old - m_new)
            p, row_part = softmax_block(s, m_new)
            m_sc[m][...] = m_new
            l_sc[m][...] = alpha * l_sc[m][...] + row_part
            a_sc[m][...] = alpha * a_sc[m][...] + jnp.dot(p, vc, preferred_element_type=F32)
        return carry

    lax.fori_loop(0, nk - 1, body, 0)

    lam = (jnp.exp(jnp.sum(lq1_ref[...] * lk1_ref[...], axis=-1, keepdims=True))
           - jnp.exp(jnp.sum(lq2_ref[...] * lk2_ref[...], axis=-1, keepdims=True)) + LAM_INIT)
    o = (a_sc0[...] / jnp.sum(l_sc0[...], axis=1, keepdims=True)
         - lam * (a_sc1[...] / jnp.sum(l_sc1[...], axis=1, keepdims=True)))
    o_ref[...] = (_rms(o, g_ref[...]) * (1.0 - LAM_INIT)).astype(BF16)


def _attention(qkv, row_off, batch, seq, lams, subln_g, tq):
    assert seq <= 64 * 64 and seq % tq == 0 and tq % 128 == 0
    nq = seq // tq
    qb0 = row_off // tq
    kb0 = row_off // seq
    vec = pl.BlockSpec((1, HEAD_DIM), lambda b, h, i: (0, 0))
    stat = pltpu.VMEM((tq, 2 * HEAD_DIM), F32)
    return pl.pallas_call(
        functools.partial(_attn_kernel, seq, tq),
        grid=(batch, N_HEADS, nq),
        in_specs=[vec, vec, vec, vec,
                  pl.BlockSpec((1, 2 * HEAD_DIM), lambda b, h, i: (0, 0)),
                  pl.BlockSpec((tq, 2 * HEAD_DIM), lambda b, h, i: (qb0 + b * nq + i, h)),
                  pl.BlockSpec((seq, 2 * HEAD_DIM), lambda b, h, i: (kb0 + b, N_HEADS + h)),
                  pl.BlockSpec((seq, 2 * HEAD_DIM), lambda b, h, i: (kb0 + b, 2 * N_HEADS + h))],
        out_specs=pl.BlockSpec((tq, 2 * HEAD_DIM), lambda b, h, i: (b * nq + i, h)),
        out_shape=jax.ShapeDtypeStruct((batch * seq, ATTN_WIDTH), BF16),
        scratch_shapes=[pltpu.VMEM((2, 2, seq, 2 * HEAD_DIM), BF16),
                        pltpu.VMEM((tq, tq), F32),
                        pltpu.VMEM((2, tq, 2 * HEAD_DIM), BF16),
                        stat, stat, stat, stat, stat, stat],
        compiler_params=_params(3),
        name="attention",
    )(*lams, subln_g, qkv, qkv, qkv)


def _seq_dft_kernel(seq, tm, t1c_ref, t1s_ref, t2c_ref, t2s_ref, g_ref, ab_ref, f_ref, lc_sc, ls_sc):
    b = pl.program_id(1)

    @pl.when(b == 0)
    def _():
        t2c = t2c_ref[...]
        t2s = t2s_ref[...]
        for a in range(tm // TWIDDLE_ROWS):
            c1 = t1c_ref[a:a + 1, :]
            s1 = t1s_ref[a:a + 1, :]
            rows = slice(a * TWIDDLE_ROWS, (a + 1) * TWIDDLE_ROWS)
            lc_sc[rows, :] = (c1 * t2c - s1 * t2s).astype(BF16)
            ls_sc[rows, :] = (-(s1 * t2c + c1 * t2s)).astype(BF16)

    f = (jnp.dot(lc_sc[...], ab_ref[:, :FOURIER_WIDTH], preferred_element_type=F32)
         + jnp.dot(ls_sc[...], ab_ref[:, FOURIER_WIDTH:], preferred_element_type=F32))
    f = f * (1.0 / math.sqrt(seq * GROUP))
    f_ref[...] = _rms(f, g_ref[...]).astype(BF16)


def _twiddle_tables(seq):
    k = jnp.arange(seq, dtype=I32)[None, :]
    a = (jnp.arange(seq // TWIDDLE_ROWS, dtype=I32) * TWIDDLE_ROWS)[:, None]
    b = jnp.arange(TWIDDLE_ROWS, dtype=I32)[:, None]
    ang1 = ((a * k) % seq).astype(F32) * (2.0 * math.pi / seq)
    ang2 = ((b * k) % seq).astype(F32) * (2.0 * math.pi / seq)
    return jnp.cos(ang1), jnp.sin(ang1), jnp.cos(ang2), jnp.sin(ang2)


def _seq_dft(ab, row_off, batch, seq, fourier_g, tm):
    t1c, t1s, t2c, t2s = _twiddle_tables(seq)
    nr = seq // tm
    na = tm // TWIDDLE_ROWS
    kb0 = row_off // seq
    t1 = pl.BlockSpec((na, seq), lambda i, b: (i, 0))
    t2 = pl.BlockSpec((TWIDDLE_ROWS, seq), lambda i, b: (0, 0))
    return pl.pallas_call(
        functools.partial(_seq_dft_kernel, seq, tm),
        grid=(nr, batch),
        in_specs=[t1, t1, t2, t2,
                  pl.BlockSpec((1, FOURIER_WIDTH), lambda i, b: (0, 0)),
                  pl.BlockSpec((seq, 2 * FOURIER_WIDTH), lambda i, b: (kb0 + b, 0))],
        out_specs=pl.BlockSpec((tm, FOURIER_WIDTH), lambda i, b: (b * nr + i, 0)),
        out_shape=jax.ShapeDtypeStruct((batch * seq, FOURIER_WIDTH), BF16),
        scratch_shapes=[pltpu.VMEM((tm, seq), BF16), pltpu.VMEM((tm, seq), BF16)],
        compiler_params=_params(2),
        name="seq_dft",
    )(t1c, t1s, t2c, t2s, fourier_g, ab)


def _out_route_kernel(nb_a, tm, xa_ref, xb_ref, oa_ref, ob_ref, fa_ref, fb_ref, wo_ref, g_ref,
                      rw_ref, rb_ref, x1_ref, h2_ref, e_ref, r_ref, gt_ref, cnt_ref, carry_sc):
    i = pl.program_id(0)
    first = i < nb_a

    @pl.when(i == 0)
    def _():
        carry_sc[...] = jnp.zeros(carry_sc.shape, F32)

    x = jnp.where(first, xa_ref[...], xb_ref[...])
    o = jnp.where(first, oa_ref[...], ob_ref[...])
    f = jnp.where(first, fa_ref[...], fb_ref[...])
    x1 = (x + jnp.dot(o, wo_ref[:ATTN_WIDTH, :], preferred_element_type=F32)
          + jnp.dot(f, wo_ref[ATTN_WIDTH:, :], preferred_element_type=F32))
    x1_ref[...] = x1
    h2 = _rms(x1, g_ref[...])
    h2_ref[...] = h2

    lg = lax.dot_general(rw_ref[...], h2.astype(BF16), (((1,), (1,)), ((), ())),
                         preferred_element_type=F32) + rb_ref[...]
    eidx = lax.broadcasted_iota(I32, lg.shape, 0)
    work = lg
    vals, hots = [], []
    for k in range(TOP_K):
        mx = jnp.max(work, axis=0, keepdims=True)
        sel = jnp.min(jnp.where(work == mx, eidx, N_EXPERTS), axis=0, keepdims=True)
        hot = eidx == sel
        work = jnp.where(hot, -jnp.inf, work)
        vals.append(mx)
        hots.append(hot)
        e_ref[0, k:k + 1, :] = sel
    ex = [jnp.exp(v - vals[0]) for v in vals]
    den = ex[0] + ex[1] + ex[2] + ex[3]
    for k in range(TOP_K):
        gt_ref[0, k:k + 1, :] = ex[k] / den

    multi = jnp.zeros(lg.shape, F32)
    for k in range(TOP_K):
        multi = multi + hots[k].astype(F32)
    tri = (lax.broadcasted_iota(I32, (tm, tm), 0) < lax.broadcasted_iota(I32, (tm, tm), 1)).astype(BF16)
    base = jnp.dot(multi.astype(BF16), tri, preferred_element_type=F32) + carry_sc[...]
    for k in range(TOP_K):
        r_ref[0, k:k + 1, :] = jnp.sum(jnp.where(hots[k], base, 0.0), axis=0, keepdims=True).astype(I32)
    carry_sc[...] = carry_sc[...] + jnp.sum(multi, axis=1, keepdims=True)
    cnt_ref[...] = carry_sc[...].astype(I32)


def _out_route(xa, xb, oa, ob, fa, fb, wo_bf, g, rw_t_bf, rb_col, tm):
    na, nb = xa.shape[0], xb.shape[0]
    nba, nbb = na // tm, nb // tm
    n, nblk = na + nb, nba + nbb

    def sel_a(w):
        return pl.BlockSpec((tm, w), lambda i: (jnp.minimum(i, nba - 1), 0))

    def sel_b(w):
        return pl.BlockSpec((tm, w), lambda i: (jnp.maximum(i - nba, 0), 0))

    slot = pl.BlockSpec((1, TOP_K, tm), lambda i: (i, 0, 0))
    return pl.pallas_call(
        functools.partial(_out_route_kernel, nba, tm),
        grid=(nblk,),
        in_specs=[sel_a(D_MODEL), sel_b(D_MODEL), sel_a(ATTN_WIDTH), sel_b(ATTN_WIDTH),
                  sel_a(FOURIER_WIDTH), sel_b(FOURIER_WIDTH),
                  pl.BlockSpec((D_MODEL, D_MODEL), lambda i: (0, 0)),
                  pl.BlockSpec((1, D_MODEL), lambda i: (0, 0)),
                  pl.BlockSpec((N_EXPERTS, D_MODEL), lambda i: (0, 0)),
                  pl.BlockSpec((N_EXPERTS, 1), lambda i: (0, 0))],
        out_specs=[pl.BlockSpec((tm, D_MODEL), lambda i: (i, 0)),
                   pl.BlockSpec((tm, D_MODEL), lambda i: (i, 0)),
                   slot, slot, slot,
                   pl.BlockSpec((N_EXPERTS, 1), lambda i: (0, 0))],
        out_shape=[jax.ShapeDtypeStruct((n, D_MODEL), F32),
                   jax.ShapeDtypeStruct((n, D_MODEL), F32),
                   jax.ShapeDtypeStruct((nblk, TOP_K, tm), I32),
                   jax.ShapeDtypeStruct((nblk, TOP_K, tm), I32),
                   jax.ShapeDtypeStruct((nblk, TOP_K, tm), F32),
                   jax.ShapeDtypeStruct((N_EXPERTS, 1), I32)],
        scratch_shapes=[pltpu.VMEM((N_EXPERTS, 1), F32)],
        compiler_params=_params(1),
        name="out_route",
    )(xa, xb, oa, ob, fa, fb, wo_bf, g, rw_t_bf, rb_col)


def _dest_kernel(ps_ref, e_ref, r_ref, d_ref):
    e = e_ref[...]
    acc = jnp.zeros(e.shape, I32)
    for x in range(N_EXPERTS):
        acc = jnp.where(e == x, ps_ref[x], acc)
    d_ref[...] = acc + r_ref[...]


def _dest(pad_starts, top_e, rank):
    nblk, _, tm = top_e.shape
    slot = pl.BlockSpec((1, TOP_K, tm), lambda i: (i, 0, 0))
    return pl.pallas_call(
        _dest_kernel,
        grid=(nblk,),
        in_specs=[pl.BlockSpec(memory_space=pltpu.SMEM), slot, slot],
        out_specs=slot,
        out_shape=jax.ShapeDtypeStruct(top_e.shape, I32),
        compiler_params=_params(1),
        name="dest",
    )(pad_starts, top_e, rank)


def _dispatch_kernel(tm, tme, n_zero, zb_ref, d_ref, h_ref, xs_ref, zeros_sc, sem, zsem):
    i = pl.program_id(0)

    @pl.when(i == 0)
    def _():
        zeros_sc[...] = jnp.zeros(zeros_sc.shape, F32)

        def zero_copy(j):
            return pltpu.make_async_copy(zeros_sc, xs_ref.at[pl.ds(zb_ref[j] * tme, tme)], zsem)

        for j in range(n_zero):
            pl.when(zb_ref[j] >= 0)(lambda j=j: zero_copy(j).start())
        for j in range(n_zero):
            pl.when(zb_ref[j] >= 0)(lambda j=j: zero_copy(j).wait())

    def body(t, carry):
        for k in range(TOP_K):
            pltpu.make_async_copy(h_ref.at[pl.ds(t, 1)], xs_ref.at[pl.ds(d_ref[0, k, t], 1)], sem).start()
        return carry

    lax.fori_loop(0, tm, body, 0)
    for k in range(TOP_K):
        pltpu.make_async_copy(h_ref, xs_ref.at[pl.ds(0, tm)], sem).wait()


def _dispatch(zero_blocks, dest, h2, n_rows, tme):
    nblk, _, tm = dest.shape
    n_zero = zero_blocks.shape[0]
    return pl.pallas_call(
        functools.partial(_dispatch_kernel, tm, tme, n_zero),
        grid_spec=pltpu.PrefetchScalarGridSpec(
            num_scalar_prefetch=1,
            grid=(nblk,),
            in_specs=[pl.BlockSpec((1, TOP_K, tm), lambda i, zb: (i, 0, 0), memory_space=pltpu.SMEM),
                      pl.BlockSpec((tm, D_MODEL), lambda i, zb: (i, 0))],
            out_specs=pl.BlockSpec(memory_space=pl.ANY),
            scratch_shapes=[pltpu.VMEM((tme, D_MODEL), F32), pltpu.SemaphoreType.DMA(()),
                            pltpu.SemaphoreType.DMA(())]),
        out_shape=jax.ShapeDtypeStruct((n_rows, D_MODEL), F32),
        compiler_params=_params(1),
        name="dispatch",
    )(zero_blocks, dest, h2)


def _experts_kernel(be_ref, nu_ref, xs_ref, wu_ref, bu_ref, wd_ref, bd_ref, ys_ref, a_sc):
    i = pl.program_id(0)

    @pl.when(i < nu_ref[0])
    def _():
        x = xs_ref[...].astype(BF16)
        cw = 256
        for c in range(D_EXPERT // cw):
            glu = jnp.dot(x, wu_ref[0, :, c * cw:(c + 1) * cw], preferred_element_type=F32) \
                + bu_ref[0, :, c * cw:(c + 1) * cw]
            lin = jnp.dot(x, wu_ref[0, :, D_EXPERT + c * cw:D_EXPERT + (c + 1) * cw],
                          preferred_element_type=F32) + bu_ref[0, :, D_EXPERT + c * cw:D_EXPERT + (c + 1) * cw]
            glu = jnp.minimum(glu, SWIGLU_LIMIT)
            lin = jnp.clip(lin, -SWIGLU_LIMIT, SWIGLU_LIMIT)
            a = glu * jax.nn.sigmoid(SWIGLU_ALPHA * glu) * (lin + 1.0)
            a_sc[:, c * cw:(c + 1) * cw] = a.astype(BF16)
        ys_ref[...] = jnp.dot(a_sc[...], wd_ref[0], preferred_element_type=F32) + bd_ref[0]

    @pl.when(i >= nu_ref[0])
    def _():
        ys_ref[...] = jnp.zeros(ys_ref.shape, F32)


def _experts(block_e, n_used, xs, wu_bf, bu, wd_bf, bd, tme):
    n_rows = xs.shape[0]
    nb = n_rows // tme

    def row_map(i, be, nu):
        return (jnp.minimum(i, nu[0] - 1), 0)

    def out_map(i, be, nu):
        return (i, 0)

    def exp_map(i, be, nu):
        return (be[i], 0, 0)

    return pl.pallas_call(
        _experts_kernel,
        grid_spec=pltpu.PrefetchScalarGridSpec(
            num_scalar_prefetch=2,
            grid=(nb,),
            in_specs=[pl.BlockSpec((tme, D_MODEL), row_map),
                      pl.BlockSpec((1, D_MODEL, 2 * D_EXPERT), exp_map),
                      pl.BlockSpec((1, 1, 2 * D_EXPERT), exp_map),
                      pl.BlockSpec((1, D_EXPERT, D_MODEL), exp_map),
                      pl.BlockSpec((1, 1, D_MODEL), exp_map)],
            out_specs=pl.BlockSpec((tme, D_MODEL), out_map),
            scratch_shapes=[pltpu.VMEM((tme, D_EXPERT), BF16)]),
        out_shape=jax.ShapeDtypeStruct((n_rows, D_MODEL), F32),
        compiler_params=_params(1),
        name="experts",
    )(block_e, n_used, xs, wu_bf, bu, wd_bf, bd)


def _final_kernel(tm, d_ref, x1_ref, gt_ref, p_ref, gp_ref, wg_ref, wp_ref, gf_ref, ys_ref, y_ref,
                  buf, sem):
    def body(t, carry):
        for k in range(TOP_K):
            pltpu.make_async_copy(ys_ref.at[pl.ds(d_ref[0, k, t], 1)], buf.at[k, pl.ds(t, 1)], sem).start()
        return carry

    lax.fori_loop(0, tm, body, 0)
    for k in range(TOP_K):
        pltpu.make_async_copy(ys_ref.at[pl.ds(0, tm)], buf.at[k], sem).wait()

    x2 = x1_ref[...]
    for k in range(TOP_K):
        x2 = x2 + buf[k] * gt_ref[:, k:k + 1]
    gate = jax.nn.sigmoid(jnp.dot(_rms(x2, gp_ref[...]).astype(BF16), wg_ref[...],
                                  preferred_element_type=F32))
    x3 = x2 + gate * jnp.dot(p_ref[...].astype(BF16), wp_ref[...], preferred_element_type=F32)
    y_ref[...] = _rms(x3, gf_ref[...])


def _final(dest, x1, gates_tok, p, g_ple, wg_bf, wp_bf, g_fin, ys, row_off):
    tm = dest.shape[2]
    n_tok = p.shape[0]
    b0 = row_off // tm
    return pl.pallas_call(
        functools.partial(_final_kernel, tm),
        grid=(n_tok // tm,),
        in_specs=[pl.BlockSpec((1, TOP_K, tm), lambda i: (b0 + i, 0, 0), memory_space=pltpu.SMEM),
                  pl.BlockSpec((tm, D_MODEL), lambda i: (b0 + i, 0)),
                  pl.BlockSpec((tm, TOP_K), lambda i: (b0 + i, 0)),
                  pl.BlockSpec((tm, PLE_DIM), lambda i: (i, 0)),
                  pl.BlockSpec((1, D_MODEL), lambda i: (0, 0)),
                  pl.BlockSpec((D_MODEL, D_MODEL), lambda i: (0, 0)),
                  pl.BlockSpec((PLE_DIM, D_MODEL), lambda i: (0, 0)),
                  pl.BlockSpec((1, D_MODEL), lambda i: (0, 0)),
                  pl.BlockSpec(memory_space=pl.ANY)],
        out_specs=pl.BlockSpec((tm, D_MODEL), lambda i: (i, 0)),
        out_shape=jax.ShapeDtypeStruct((n_tok, D_MODEL), F32),
        scratch_shapes=[pltpu.VMEM((TOP_K, tm, D_MODEL), F32), pltpu.SemaphoreType.DMA(())],
        compiler_params=_params(1),
        name="final",
    )(dest, x1, gates_tok, p, g_ple, wg_bf, wp_bf, g_fin, ys)


def _tiles(seq_a, seq_b):
    smin = min(seq_a, seq_b)
    return dict(tm=min(512, smin), tq=min(512, smin), tk=min(512, smin), tf=min(512, smin),
                tr=min(512, smin), tme=min(512, smin))


def _forward(x_a, x_b, p_a, p_b, norm_mix, w_in, lambda_q1, lambda_k1, lambda_q2, lambda_k2, subln_g,
             fourier_norm, w_out, norm_ffn, router_w, router_b, w_up, b_up, w_down, b_down, norm_ple,
             w_ple_gate, w_ple_proj, final_norm):
    (ba, sa, _), (bb, sb, _) = x_a.shape, x_b.shape
    na, nb = ba * sa, bb * sb
    n = na + nb
    t = _tiles(sa, sb)
    xa2, xb2 = x_a.reshape(na, D_MODEL), x_b.reshape(nb, D_MODEL)

    c = jnp.arange(GROUP, dtype=I32)
    ang = ((c[:, None] * c[None, :]) % GROUP).astype(F32) * (2.0 * math.pi / GROUP)
    dft = jnp.concatenate([jnp.cos(ang), jnp.sin(ang)], axis=1).astype(BF16)
    row = lambda v: v.reshape(1, -1).astype(F32)

    qkv, ab = _in_proj(xa2, xb2, row(norm_mix[0]), w_in[0].astype(BF16), dft, t["tm"])

    lams = (row(lambda_q1[0]), row(lambda_k1[0]), row(lambda_q2[0]), row(lambda_k2[0]))
    o_a = _attention(qkv, 0, ba, sa, lams, row(subln_g[0]), t["tq"])
    o_b = _attention(qkv, na, bb, sb, lams, row(subln_g[0]), t["tq"])
    f_a = _seq_dft(ab, 0, ba, sa, row(fourier_norm[0]), t["tf"])
    f_b = _seq_dft(ab, na, bb, sb, row(fourier_norm[0]), t["tf"])

    tr = t["tr"]
    x1, h2, top_e, rank, gates, counts = _out_route(
        xa2, xb2, o_a, o_b, f_a, f_b, w_out[0].astype(BF16), row(norm_ffn[0]),
        router_w[0].T.astype(BF16), router_b[0].reshape(N_EXPERTS, 1).astype(F32), tr)

    tme = t["tme"]
    counts = counts.reshape(N_EXPERTS)
    blocks = (counts + tme - 1) // tme
    blk_ends = jnp.cumsum(blocks)
    pad_starts = ((blk_ends - blocks) * tme).astype(I32)
    n_blocks = (n * TOP_K) // tme + N_EXPERTS
    n_used = blk_ends[-1:].astype(I32)
    bidx = jnp.minimum(jnp.arange(n_blocks, dtype=I32), n_used[0] - 1)
    block_e = jnp.sum((bidx[:, None] >= blk_ends[None, :]).astype(I32), axis=1).astype(I32)

    tail = n_used[0] + jnp.arange(N_EXPERTS, dtype=I32)
    zero_blocks = jnp.concatenate([jnp.where(blocks > 0, blk_ends - 1, -1).astype(I32),
                                   jnp.where(tail < n_blocks, tail, -1)])

    dest = _dest(pad_starts, top_e, rank)
    xs = _dispatch(zero_blocks, dest, h2, n_blocks * tme, tme)
    ys = _experts(block_e, n_used, xs, w_up[0].astype(BF16), b_up[0].reshape(N_EXPERTS, 1, -1),
                  w_down[0].astype(BF16), b_down[0].reshape(N_EXPERTS, 1, -1), tme)

    gates_tok = jnp.transpose(gates, (0, 2, 1)).reshape(n, TOP_K)
    fin = functools.partial(_final, dest, x1, gates_tok)
    wg, wp = w_ple_gate[0].astype(BF16), w_ple_proj[0].astype(BF16)
    y_a = fin(p_a[0].reshape(na, PLE_DIM), row(norm_ple[0]), wg, wp, row(final_norm), ys, 0)
    y_b = fin(p_b[0].reshape(nb, PLE_DIM), row(norm_ple[0]), wg, wp, row(final_norm), ys, na)
    return y_a.reshape(ba, sa, D_MODEL), y_b.reshape(bb, sb, D_MODEL)


def kernel(x_prompt, x_sample, p_prompt, p_sample, norm_mix, w_in, lambda_q1, lambda_k1, lambda_q2, lambda_k2, subln_g, fourier_norm, w_out, norm_ffn, router_w, router_b, w_up, b_up, w_down, b_down, norm_ple, w_ple_gate, w_ple_proj, final_norm):
    return _forward(x_prompt, x_sample, p_prompt, p_sample, norm_mix, w_in, lambda_q1, lambda_k1,
                    lambda_q2, lambda_k2, subln_g, fourier_norm, w_out, norm_ffn, router_w, router_b,
                    w_up, b_up, w_down, b_down, norm_ple, w_ple_gate, w_ple_proj, final_norm)
```

```python
import functools
import math

import jax
import jax.numpy as jnp
from jax import lax
from jax.experimental import pallas as pl
from jax.experimental.pallas import tpu as pltpu

F32 = jnp.float32
BF16 = jnp.bfloat16
I32 = jnp.int32

D_MODEL = 1024
ATTN_WIDTH = 512
FOURIER_WIDTH = 512
HEAD_DIM = 64
N_HEADS = 4
N_GROUPS = 4
GROUP = 128
N_EXPERTS = 32
TOP_K = 4
D_EXPERT = 1024
SWIGLU_LIMIT = 7.0
SWIGLU_ALPHA = 1.702
PLE_DIM = 256
EPS = 1e-6
LAM_INIT = 0.8 - 0.6 * math.exp(-0.3 * 0)
TWIDDLE_ROWS = 64
ROW_UNROLL = 8

VMEM_LIMIT = 56 * 1024 * 1024


def _params(n_axes):
    return pltpu.CompilerParams(dimension_semantics=("arbitrary",) * n_axes,
                                vmem_limit_bytes=VMEM_LIMIT)


def _rms(x, g):
    return x * lax.rsqrt(jnp.mean(x * x, axis=-1, keepdims=True) + EPS) * g


def _in_proj_kernel(nb_a, xa_ref, xb_ref, g_ref, w_ref, dft_ref, qkv_ref, ab_ref):
    i = pl.program_id(0)
    x = jnp.where(i < nb_a, xa_ref[...], xb_ref[...])
    h = _rms(x, g_ref[...]).astype(BF16)
    for c in range(3):
        pc = jnp.dot(h, w_ref[:, c * ATTN_WIDTH:(c + 1) * ATTN_WIDTH], preferred_element_type=F32)
        if c == 0:
            pc = pc * (HEAD_DIM ** -0.5)
        qkv_ref[:, c * ATTN_WIDTH:(c + 1) * ATTN_WIDTH] = pc.astype(BF16)
    u = jnp.dot(h, w_ref[:, 3 * ATTN_WIDTH:], preferred_element_type=F32).astype(BF16)
    for g in range(N_GROUPS):
        r = jnp.dot(u[:, g * GROUP:(g + 1) * GROUP], dft_ref[...], preferred_element_type=F32)
        ab_ref[:, g * GROUP:(g + 1) * GROUP] = r[:, :GROUP].astype(BF16)
        ab_ref[:, FOURIER_WIDTH + g * GROUP:FOURIER_WIDTH + (g + 1) * GROUP] = r[:, GROUP:].astype(BF16)


def _in_proj(xa, xb, g, w_bf, dft_bf, tm):
    na, nb = xa.shape[0], xb.shape[0]
    nba, nbb = na // tm, nb // tm
    n = na + nb
    return pl.pallas_call(
        functools.partial(_in_proj_kernel, nba),
        grid=(nba + nbb,),
        in_specs=[
            pl.BlockSpec((tm, D_MODEL), lambda i: (jnp.minimum(i, nba - 1), 0)),
            pl.BlockSpec((tm, D_MODEL), lambda i: (jnp.maximum(i - nba, 0), 0)),
            pl.BlockSpec((1, D_MODEL), lambda i: (0, 0)),
            pl.BlockSpec((D_MODEL, 4 * ATTN_WIDTH), lambda i: (0, 0)),
            pl.BlockSpec((GROUP, 2 * GROUP), lambda i: (0, 0)),
        ],
        out_specs=[
            pl.BlockSpec((tm, 3 * ATTN_WIDTH), lambda i: (i, 0)),
            pl.BlockSpec((tm, 2 * FOURIER_WIDTH), lambda i: (i, 0)),
        ],
        out_shape=[jax.ShapeDtypeStruct((n, 3 * ATTN_WIDTH), BF16),
                   jax.ShapeDtypeStruct((n, 2 * FOURIER_WIDTH), BF16)],
        compiler_params=_params(1),
        name="in_proj",
    )(xa, xb, g, w_bf, dft_bf)


def _alibi_lanes(m, lane):
    data = (lane < HEAD_DIM) if m == 0 else (lane >= HEAD_DIM)
    return data, (HEAD_DIM if m == 0 else 0)


def _attn_kernel(seq, tq, lq1_ref, lk1_ref, lq2_ref, lk2_ref, g_ref, q_ref, k_ref, v_ref, o_ref,
                 kaug_sc, dbias_sc, qa_sc, m_sc0, l_sc0, a_sc0, m_sc1, l_sc1, a_sc1):
    h = pl.program_id(1)
    qi = pl.program_id(2)
    nk = seq // tq
    slope = jnp.where(h == 0, 0.25, jnp.where(h == 1, 0.0625, jnp.where(h == 2, 0.015625, 0.00390625)))
    slope = slope.astype(F32)
    m_sc, l_sc, a_sc = (m_sc0, m_sc1), (l_sc0, l_sc1), (a_sc0, a_sc1)
    nt = (((1,), (1,)), ((), ()))
    lane = lax.broadcasted_iota(I32, (tq, 2 * HEAD_DIM), 1)
    rows = lax.broadcasted_iota(I32, (tq, 2 * HEAD_DIM), 0)

    def features(fb, f0, f1, f2, f3):
        return jnp.where(lane == fb, f0, jnp.where(lane == fb + 1, f1, jnp.where(
            lane == fb + 2, f2, jnp.where(lane == fb + 3, f3, 0.0))))

    @pl.when(qi == 0)
    def _():
        r = lax.broadcasted_iota(I32, (tq, tq), 0)
        c = lax.broadcasted_iota(I32, (tq, tq), 1)
        dbias_sc[...] = -slope * jnp.abs(r - c).astype(F32)

        def build(j, carry):
            start = pl.multiple_of(j * tq, tq)
            kc = k_ref[pl.ds(start, tq), :]
            pos = start + rows
            jh = (pos >> 6).astype(F32)
            jl = (pos & 63).astype(F32)
            for m in range(2):
                data, fb = _alibi_lanes(m, lane)
                left = features(fb, -64.0 * slope, -slope, 64.0 * slope * jh, slope * jl)
                kaug_sc[m, 0, pl.ds(start, tq), :] = jnp.where(data, kc, left.astype(BF16))
                kaug_sc[m, 1, pl.ds(start, tq), :] = jnp.where(data, kc, (-left).astype(BF16))
            return carry

        lax.fori_loop(0, nk, build, 0)

    q = q_ref[...]
    ipos = qi * tq + rows
    q_plain = []
    for m in range(2):
        data, fb = _alibi_lanes(m, lane)
        feat = features(fb, (ipos >> 6).astype(F32), (ipos & 63).astype(F32), 1.0, 1.0)
        qa_sc[m] = jnp.where(data, q, feat.astype(BF16))
        q_plain.append(jnp.where(data, q, jnp.zeros_like(q)))

    def softmax_block(s, m_new):
        parts = [jnp.exp(s[:, g * 128:(g + 1) * 128] - m_new) for g in range(tq // 128)]
        row_part = parts[0]
        for p in parts[1:]:
            row_part = row_part + p
        return jnp.concatenate(parts, axis=1).astype(BF16), row_part

    dstart = pl.multiple_of(qi * tq, tq)
    vd = v_ref[pl.ds(dstart, tq), :]
    for m in range(2):
        s = lax.dot_general(q_plain[m], kaug_sc[m, 0, pl.ds(dstart, tq), :], nt,
                            preferred_element_type=F32) + dbias_sc[...]
        m_new = jnp.broadcast_to(jnp.max(s, axis=1, keepdims=True), (tq, 128))
        p, row_part = softmax_block(s, m_new)
        m_sc[m][...] = m_new
        l_sc[m][...] = row_part
        a_sc[m][...] = jnp.dot(p, vd, preferred_element_type=F32)

    def body(jj, carry):
        j = jj + (jj >= qi).astype(I32)
        side = (j > qi).astype(I32)
        start = pl.multiple_of(j * tq, tq)
        vc = v_ref[pl.ds(start, tq), :]
        for m in range(2):
            s = lax.dot_general(qa_sc[m], kaug_sc[m, side, pl.ds(start, tq), :], nt,
                                preferred_element_type=F32)
            m_old = m_sc[m][...]
            m_new = jnp.maximum(m_old, jnp.max(s, axis=1, keepdims=True))
            alpha = jnp.exp(m_old - m_new)
            p, row_part = softmax_block(s, m_new)
            m_sc[m][...] = m_new
            l_sc[m][...] = alpha * l_sc[m][...] + row_part
            a_sc[m][...] = alpha * a_sc[m][...] + jnp.dot(p, vc, preferred_element_type=F32)
        return carry

    lax.fori_loop(0, nk - 1, body, 0, unroll=True)

    lam =(jnp.exp(jnp.sum(lq1_ref[...] * lk1_ref[...], axis=-1, keepdims=True))
           - jnp.exp(jnp.sum(lq2_ref[...] * lk2_ref[...], axis=-1, keepdims=True)) + LAM_INIT)
    o = (a_sc0[...] / jnp.sum(l_sc0[...], axis=1, keepdims=True)
         - lam * (a_sc1[...] / jnp.sum(l_sc1[...], axis=1, keepdims=True)))
    o_ref[...] = (_rms(o, g_ref[...]) * (1.0 - LAM_INIT)).astype(BF16)


def _attention(qkv, row_off, batch, seq, lams, subln_g, tq):
    assert seq <= 64 * 64 and seq % tq == 0 and tq % 128 == 0
    nq = seq // tq
    qb0 = row_off // tq
    kb0 = row_off // seq
    vec = pl.BlockSpec((1, HEAD_DIM), lambda b, h, i: (0, 0))
    stat = pltpu.VMEM((tq, 2 * HEAD_DIM), F32)
    return pl.pallas_call(
        functools.partial(_attn_kernel, seq, tq),
        grid=(batch, N_HEADS, nq),
        in_specs=[vec, vec, vec, vec,
                  pl.BlockSpec((1, 2 * HEAD_DIM), lambda b, h, i: (0, 0)),
                  pl.BlockSpec((tq, 2 * HEAD_DIM), lambda b, h, i: (qb0 + b * nq + i, h)),
                  pl.BlockSpec((seq, 2 * HEAD_DIM), lambda b, h, i: (kb0 + b, N_HEADS + h)),
                  pl.BlockSpec((seq, 2 * HEAD_DIM), lambda b, h, i: (kb0 + b, 2 * N_HEADS + h))],
        out_specs=pl.BlockSpec((tq, 2 * HEAD_DIM), lambda b, h, i: (b * nq + i, h)),
        out_shape=jax.ShapeDtypeStruct((batch * seq, ATTN_WIDTH), BF16),
        scratch_shapes=[pltpu.VMEM((2, 2, seq, 2 * HEAD_DIM), BF16),
                        pltpu.VMEM((tq, tq), F32),
                        pltpu.VMEM((2, tq, 2 * HEAD_DIM), BF16),
                        stat, stat, stat, stat, stat, stat],
        compiler_params=_params(3),
        name="attention",
    )(*lams, subln_g, qkv, qkv, qkv)


def _seq_dft_kernel(seq, tm, t1c_ref, t1s_ref, t2c_ref, t2s_ref, g_ref, ab_ref, f_ref, lc_sc, ls_sc):
    b = pl.program_id(1)

    @pl.when(b == 0)
    def _():
        t2c = t2c_ref[...]
        t2s = t2s_ref[...]
        for a in range(tm // TWIDDLE_ROWS):
            c1 = t1c_ref[a:a + 1, :]
            s1 = t1s_ref[a:a + 1, :]
            rows = slice(a * TWIDDLE_ROWS, (a + 1) * TWIDDLE_ROWS)
            lc_sc[rows, :] = (c1 * t2c - s1 * t2s).astype(BF16)
            ls_sc[rows, :] = (-(s1 * t2c + c1 * t2s)).astype(BF16)

    f = (jnp.dot(lc_sc[...], ab_ref[:, :FOURIER_WIDTH], preferred_element_type=F32)
         + jnp.dot(ls_sc[...], ab_ref[:, FOURIER_WIDTH:], preferred_element_type=F32))
    f = f * (1.0 / math.sqrt(seq * GROUP))
    f_ref[...] = _rms(f, g_ref[...]).astype(BF16)


def _twiddle_tables(seq):
    k = jnp.arange(seq, dtype=I32)[None, :]
    a = (jnp.arange(seq // TWIDDLE_ROWS, dtype=I32) * TWIDDLE_ROWS)[:, None]
    b = jnp.arange(TWIDDLE_ROWS, dtype=I32)[:, None]
    ang1 = ((a * k) % seq).astype(F32) * (2.0 * math.pi / seq)
    ang2 = ((b * k) % seq).astype(F32) * (2.0 * math.pi / seq)
    return jnp.cos(ang1), jnp.sin(ang1), jnp.cos(ang2), jnp.sin(ang2)


def _seq_dft(ab, row_off, batch, seq, fourier_g, tm):
    t1c, t1s, t2c, t2s = _twiddle_tables(seq)
    nr = seq // tm
    na = tm // TWIDDLE_ROWS
    kb0 = row_off // seq
    t1 = pl.BlockSpec((na, seq), lambda i, b: (i, 0))
    t2 = pl.BlockSpec((TWIDDLE_ROWS, seq), lambda i, b: (0, 0))
    return pl.pallas_call(
        functools.partial(_seq_dft_kernel, seq, tm),
        grid=(nr, batch),
        in_specs=[t1, t1, t2, t2,
                  pl.BlockSpec((1, FOURIER_WIDTH), lambda i, b: (0, 0)),
                  pl.BlockSpec((seq, 2 * FOURIER_WIDTH), lambda i, b: (kb0 + b, 0))],
        out_specs=pl.BlockSpec((tm, FOURIER_WIDTH), lambda i, b: (b * nr + i, 0)),
        out_shape=jax.ShapeDtypeStruct((batch * seq, FOURIER_WIDTH), BF16),
        scratch_shapes=[pltpu.VMEM((tm, seq), BF16), pltpu.VMEM((tm, seq), BF16)],
        compiler_params=_params(2),
        name="seq_dft",
    )(t1c, t1s, t2c, t2s, fourier_g, ab)


def _out_route_kernel(nb_a, tm, xa_ref, xb_ref, oa_ref, ob_ref, fa_ref, fb_ref, wo_ref, g_ref,
                      rw_ref, rb_ref, x1_ref, h2_ref, e_ref, r_ref, gt_ref, cnt_ref, carry_sc):
    i = pl.program_id(0)
    first = i < nb_a

    @pl.when(i == 0)
    def _():
        carry_sc[...] = jnp.zeros(carry_sc.shape, F32)

    x = jnp.where(first, xa_ref[...], xb_ref[...])
    o = jnp.where(first, oa_ref[...], ob_ref[...])
    f = jnp.where(first, fa_ref[...], fb_ref[...])
    x1 = (x + jnp.dot(o, wo_ref[:ATTN_WIDTH, :], preferred_element_type=F32)
          + jnp.dot(f, wo_ref[ATTN_WIDTH:, :], preferred_element_type=F32))
    x1_ref[...] = x1
    h2 = _rms(x1, g_ref[...])
    h2_ref[...] = h2

    lg = lax.dot_general(rw_ref[...], h2.astype(BF16), (((1,), (1,)), ((), ())),
                         preferred_element_type=F32) + rb_ref[...]
    eidx = lax.broadcasted_iota(I32, lg.shape, 0)
    work = lg
    vals, hots = [], []
    for k in range(TOP_K):
        mx = jnp.max(work, axis=0, keepdims=True)
        sel = jnp.min(jnp.where(work == mx, eidx, N_EXPERTS), axis=0, keepdims=True)
        hot = eidx == sel
        work = jnp.where(hot, -jnp.inf, work)
        vals.append(mx)
        hots.append(hot)
        e_ref[0, k:k + 1, :] = sel
    ex = [jnp.exp(v - vals[0]) for v in vals]
    den = ex[0] + ex[1] + ex[2] + ex[3]
    for k in range(TOP_K):
        gt_ref[0, k:k + 1, :] = ex[k] / den

    multi = jnp.zeros(lg.shape, F32)
    for k in range(TOP_K):
        multi = multi + hots[k].astype(F32)
    tri = (lax.broadcasted_iota(I32, (tm, tm), 0) < lax.broadcasted_iota(I32, (tm, tm), 1)).astype(BF16)
    base = jnp.dot(multi.astype(BF16), tri, preferred_element_type=F32) + carry_sc[...]
    for k in range(TOP_K):
        r_ref[0, k:k + 1, :] = jnp.sum(jnp.where(hots[k], base, 0.0), axis=0, keepdims=True).astype(I32)
    carry_sc[...] = carry_sc[...] + jnp.sum(multi, axis=1, keepdims=True)
    cnt_ref[...] = carry_sc[...].astype(I32)


def _out_route(xa, xb, oa, ob, fa, fb, wo_bf, g, rw_t_bf, rb_col, tm):
    na, nb = xa.shape[0], xb.shape[0]
    nba, nbb = na // tm, nb // tm
    n, nblk = na + nb, nba + nbb

    def sel_a(w):
        return pl.BlockSpec((tm, w), lambda i: (jnp.minimum(i, nba - 1), 0))

    def sel_b(w):
        return pl.BlockSpec((tm, w), lambda i: (jnp.maximum(i - nba, 0), 0))

    slot = pl.BlockSpec((1, TOP_K, tm), lambda i: (i, 0, 0))
    return pl.pallas_call(
        functools.partial(_out_route_kernel, nba, tm),
        grid=(nblk,),
        in_specs=[sel_a(D_MODEL), sel_b(D_MODEL), sel_a(ATTN_WIDTH), sel_b(ATTN_WIDTH),
                  sel_a(FOURIER_WIDTH), sel_b(FOURIER_WIDTH),
                  pl.BlockSpec((D_MODEL, D_MODEL), lambda i: (0, 0)),
                  pl.BlockSpec((1, D_MODEL), lambda i: (0, 0)),
                  pl.BlockSpec((N_EXPERTS, D_MODEL), lambda i: (0, 0)),
                  pl.BlockSpec((N_EXPERTS, 1), lambda i: (0, 0))],
        out_specs=[pl.BlockSpec((tm, D_MODEL), lambda i: (i, 0)),
                   pl.BlockSpec((tm, D_MODEL), lambda i: (i, 0)),
                   slot, slot, slot,
                   pl.BlockSpec((N_EXPERTS, 1), lambda i: (0, 0))],
        out_shape=[jax.ShapeDtypeStruct((n, D_MODEL), F32),
                   jax.ShapeDtypeStruct((n, D_MODEL), F32),
                   jax.ShapeDtypeStruct((nblk, TOP_K, tm), I32),
                   jax.ShapeDtypeStruct((nblk, TOP_K, tm), I32),
                   jax.ShapeDtypeStruct((nblk, TOP_K, tm), F32),
                   jax.ShapeDtypeStruct((N_EXPERTS, 1), I32)],
        scratch_shapes=[pltpu.VMEM((N_EXPERTS, 1), F32)],
        compiler_params=_params(1),
        name="out_route",
    )(xa, xb, oa, ob, fa, fb, wo_bf, g, rw_t_bf, rb_col)


def _dest_kernel(ps_ref, e_ref, r_ref, d_ref):
    e = e_ref[...]
    acc = jnp.zeros(e.shape, I32)
    for x in range(N_EXPERTS):
        acc = jnp.where(e == x, ps_ref[x], acc)
    d = acc + r_ref[...]
    tm = d.shape[2]
    for k in range(TOP_K):
        d_ref[0, :, k * tm:(k + 1) * tm] = d[0, k:k + 1, :]


def _dest(pad_starts, top_e, rank):
    nblk, _, tm = top_e.shape
    slot = pl.BlockSpec((1, TOP_K, tm), lambda i: (i, 0, 0))
    return pl.pallas_call(
        _dest_kernel,
        grid=(nblk,),
        in_specs=[pl.BlockSpec(memory_space=pltpu.SMEM), slot, slot],
        out_specs=pl.BlockSpec((1, 1, TOP_K * tm), lambda i: (i, 0, 0)),
        out_shape=jax.ShapeDtypeStruct((nblk, 1, TOP_K * tm), I32),
        compiler_params=_params(1),
        name="dest",
    )(pad_starts, top_e, rank)


def _dispatch_kernel(tm, tme, n_zero, zb_ref, d_ref, h_ref, xs_ref, zeros_sc, sem, zsem):
    i = pl.program_id(0)

    @pl.when(i == 0)
    def _():
        zeros_sc[...] = jnp.zeros(zeros_sc.shape, F32)

        def zero_copy(j):
            return pltpu.make_async_copy(zeros_sc, xs_ref.at[pl.ds(zb_ref[j] * tme, tme)], zsem)

        for j in range(n_zero):
            pl.when(zb_ref[j] >= 0)(lambda j=j: zero_copy(j).start())
        for j in range(n_zero):
            pl.when(zb_ref[j] >= 0)(lambda j=j: zero_copy(j).wait())

    def body(t8, carry):
        base = pl.multiple_of(t8 * ROW_UNROLL, ROW_UNROLL)
        for u in range(ROW_UNROLL):
            for k in range(TOP_K):
                pltpu.make_async_copy(h_ref.at[pl.ds(base + u, 1)],
                                      xs_ref.at[pl.ds(d_ref[0, 0, k * tm + base + u], 1)],
                                      sem).start(priority=k % 2)
        return carry

    lax.fori_loop(0, tm // ROW_UNROLL, body, 0)
    for k in range(TOP_K):
        pltpu.make_async_copy(h_ref, xs_ref.at[pl.ds(0, tm)], sem).wait()


def _dispatch(zero_blocks, dest, h2, n_rows, tme):
    nblk, tm = dest.shape[0], dest.shape[2] // TOP_K
    n_zero = zero_blocks.shape[0]
    return pl.pallas_call(
        functools.partial(_dispatch_kernel, tm, tme, n_zero),
        grid_spec=pltpu.PrefetchScalarGridSpec(
            num_scalar_prefetch=1,
            grid=(nblk,),
            in_specs=[pl.BlockSpec((1, 1, TOP_K * tm), lambda i, zb: (i, 0, 0), memory_space=pltpu.SMEM),
                      pl.BlockSpec((tm, D_MODEL), lambda i, zb: (i, 0))],
            out_specs=pl.BlockSpec(memory_space=pl.ANY),
            scratch_shapes=[pltpu.VMEM((tme, D_MODEL), F32), pltpu.SemaphoreType.DMA(()),
                            pltpu.SemaphoreType.DMA(())]),
        out_shape=jax.ShapeDtypeStruct((n_rows, D_MODEL), F32),
        compiler_params=_params(1),
        name="dispatch",
    )(zero_blocks, dest, h2)


def _experts_kernel(be_ref, nu_ref, xs_ref, wu_ref, bu_ref, wd_ref, bd_ref, ys_ref, a_sc):
    i = pl.program_id(0)

    @pl.when(i < nu_ref[0])
    def _():
        x = xs_ref[...].astype(BF16)
        cw = 256
        for c in range(D_EXPERT // cw):
            glu = jnp.dot(x, wu_ref[0, :, c * cw:(c + 1) * cw], preferred_element_type=F32) \
                + bu_ref[0, :, c * cw:(c + 1) * cw]
            lin = jnp.dot(x, wu_ref[0, :, D_EXPERT + c * cw:D_EXPERT + (c + 1) * cw],
                          preferred_element_type=F32) + bu_ref[0, :, D_EXPERT + c * cw:D_EXPERT + (c + 1) * cw]
            glu = jnp.minimum(glu, SWIGLU_LIMIT)
            lin = jnp.clip(lin, -SWIGLU_LIMIT, SWIGLU_LIMIT)
            a = glu * jax.nn.sigmoid(SWIGLU_ALPHA * glu) * (lin + 1.0)
            a_sc[:, c * cw:(c + 1) * cw] = a.astype(BF16)
        ys_ref[...] = jnp.dot(a_sc[...], wd_ref[0], preferred_element_type=F32) + bd_ref[0]

    @pl.when(i >= nu_ref[0])
    def _():
        ys_ref[...] = jnp.zeros(ys_ref.shape, F32)


def _experts(block_e, n_used, xs, wu_bf, bu, wd_bf, bd, tme):
    n_rows = xs.shape[0]
    nb = n_rows // tme

    def row_map(i, be, nu):
        return (jnp.minimum(i, nu[0] - 1), 0)

    def out_map(i, be, nu):
        return (i, 0)

    def exp_map(i, be, nu):
        return (be[i], 0, 0)

    return pl.pallas_call(
        _experts_kernel,
        grid_spec=pltpu.PrefetchScalarGridSpec(
            num_scalar_prefetch=2,
            grid=(nb,),
            in_specs=[pl.BlockSpec((tme, D_MODEL), row_map),
                      pl.BlockSpec((1, D_MODEL, 2 * D_EXPERT), exp_map),
                      pl.BlockSpec((1, 1, 2 * D_EXPERT), exp_map),
                      pl.BlockSpec((1, D_EXPERT, D_MODEL), exp_map),
                      pl.BlockSpec((1, 1, D_MODEL), exp_map)],
            out_specs=pl.BlockSpec((tme, D_MODEL), out_map),
            scratch_shapes=[pltpu.VMEM((tme, D_EXPERT), BF16)]),
        out_shape=jax.ShapeDtypeStruct((n_rows, D_MODEL), F32),
        compiler_params=_params(1),
        name="experts",
    )(block_e, n_used, xs, wu_bf, bu, wd_bf, bd)


def _final_kernel(tm, d_ref, x1_ref, gt_ref, p_ref, gp_ref, wg_ref, wp_ref, gf_ref, ys_ref, y_ref,
                  buf, sem):
    def body(t8, carry):
        base = pl.multiple_of(t8 * ROW_UNROLL, ROW_UNROLL)
        for u in range(ROW_UNROLL):
            for k in range(TOP_K):
                pltpu.make_async_copy(ys_ref.at[pl.ds(d_ref[0, 0, k * tm + base + u], 1)],
                                      buf.at[k, pl.ds(base + u, 1)], sem).start(priority=k % 2)
        return carry

    lax.fori_loop(0, tm // ROW_UNROLL, body, 0)
    for k in range(TOP_K):
        pltpu.make_async_copy(ys_ref.at[pl.ds(0, tm)], buf.at[k], sem).wait()

    x2 = x1_ref[...]
    for k in range(TOP_K):
        x2 = x2 + buf[k] * gt_ref[:, k:k + 1]
    gate = jax.nn.sigmoid(jnp.dot(_rms(x2, gp_ref[...]).astype(BF16), wg_ref[...],
                                  preferred_element_type=F32))
    x3 = x2 + gate * jnp.dot(p_ref[...].astype(BF16), wp_ref[...], preferred_element_type=F32)
    y_ref[...] = _rms(x3, gf_ref[...])


def _final(dest, x1, gates_tok, p, g_ple, wg_bf, wp_bf, g_fin, ys, row_off):
    tm = dest.shape[2] // TOP_K
    n_tok = p.shape[0]
    b0 = row_off // tm
    return pl.pallas_call(
        functools.partial(_final_kernel, tm),
        grid=(n_tok // tm,),
        in_specs=[pl.BlockSpec((1, 1, TOP_K * tm), lambda i: (b0 + i, 0, 0), memory_space=pltpu.SMEM),
                  pl.BlockSpec((tm, D_MODEL), lambda i: (b0 + i, 0)),
                  pl.BlockSpec((tm, TOP_K), lambda i: (b0 + i, 0)),
                  pl.BlockSpec((tm, PLE_DIM), lambda i: (i, 0)),
                  pl.BlockSpec((1, D_MODEL), lambda i: (0, 0)),
                  pl.BlockSpec((D_MODEL, D_MODEL), lambda i: (0, 0)),
                  pl.BlockSpec((PLE_DIM, D_MODEL), lambda i: (0, 0)),
                  pl.BlockSpec((1, D_MODEL), lambda i: (0, 0)),
                  pl.BlockSpec(memory_space=pl.ANY)],
        out_specs=pl.BlockSpec((tm, D_MODEL), lambda i: (i, 0)),
        out_shape=jax.ShapeDtypeStruct((n_tok, D_MODEL), F32),
        scratch_shapes=[pltpu.VMEM((TOP_K, tm, D_MODEL), F32), pltpu.SemaphoreType.DMA(())],
        compiler_params=_params(1),
        name="final",
    )(dest, x1, gates_tok, p, g_ple, wg_bf, wp_bf, g_fin, ys)


def _tiles(seq_a, seq_b):
    smin = min(seq_a, seq_b)
    return dict(tm=min(512, smin), tq=min(512, smin), tk=min(512, smin), tf=min(512, smin),
                tr=min(512, smin), tme=min(512, smin))


def _forward(x_a, x_b, p_a, p_b, norm_mix, w_in, lambda_q1, lambda_k1, lambda_q2, lambda_k2, subln_g,
             fourier_norm, w_out, norm_ffn, router_w, router_b, w_up, b_up, w_down, b_down, norm_ple,
             w_ple_gate, w_ple_proj, final_norm):
    (ba, sa, _), (bb, sb, _) = x_a.shape, x_b.shape
    na, nb = ba * sa, bb * sb
    n = na + nb
    t = _tiles(sa, sb)
    xa2, xb2 = x_a.reshape(na, D_MODEL), x_b.reshape(nb, D_MODEL)

    c = jnp.arange(GROUP, dtype=I32)
    ang = ((c[:, None] * c[None, :]) % GROUP).astype(F32) * (2.0 * math.pi / GROUP)
    dft = jnp.concatenate([jnp.cos(ang), jnp.sin(ang)], axis=1).astype(BF16)
    row = lambda v: v.reshape(1, -1).astype(F32)

    qkv, ab = _in_proj(xa2, xb2, row(norm_mix[0]), w_in[0].astype(BF16), dft, t["tm"])

    lams = (row(lambda_q1[0]), row(lambda_k1[0]), row(lambda_q2[0]), row(lambda_k2[0]))
    o_a = _attention(qkv, 0, ba, sa, lams, row(subln_g[0]), t["tq"])
    o_b = _attention(qkv, na, bb, sb, lams, row(subln_g[0]), t["tq"])
    f_a = _seq_dft(ab, 0, ba, sa, row(fourier_norm[0]), t["tf"])
    f_b = _seq_dft(ab, na, bb, sb, row(fourier_norm[0]), t["tf"])

    tr = t["tr"]
    x1, h2, top_e, rank, gates, counts = _out_route(
        xa2, xb2, o_a, o_b, f_a, f_b, w_out[0].astype(BF16), row(norm_ffn[0]),
        router_w[0].T.astype(BF16), router_b[0].reshape(N_EXPERTS, 1).astype(F32), tr)

    tme = t["tme"]
    counts = counts.reshape(N_EXPERTS)
    blocks = (counts + tme - 1) // tme
    blk_ends = jnp.cumsum(blocks)
    pad_starts = ((blk_ends - blocks) * tme).astype(I32)
    n_blocks = (n * TOP_K) // tme + N_EXPERTS
    n_used = blk_ends[-1:].astype(I32)
    bidx = jnp.minimum(jnp.arange(n_blocks, dtype=I32), n_used[0] - 1)
    block_e = jnp.sum((bidx[:, None] >= blk_ends[None, :]).astype(I32), axis=1).astype(I32)

    tail = n_used[0] + jnp.arange(N_EXPERTS, dtype=I32)
    zero_blocks = jnp.concatenate([jnp.where(blocks > 0, blk_ends - 1, -1).astype(I32),
                                   jnp.where(tail < n_blocks, tail, -1)])

    dest = _dest(pad_starts, top_e, rank)
    xs = _dispatch(zero_blocks, dest, h2, n_blocks * tme, tme)
    ys = _experts(block_e, n_used, xs, w_up[0].astype(BF16), b_up[0].reshape(N_EXPERTS, 1, -1),
                  w_down[0].astype(BF16), b_down[0].reshape(N_EXPERTS, 1, -1), tme)

    gates_tok = jnp.transpose(gates, (0, 2, 1)).reshape(n, TOP_K)
    fin = functools.partial(_final, dest, x1, gates_tok)
    wg, wp = w_ple_gate[0].astype(BF16), w_ple_proj[0].astype(BF16)
    y_a = fin(p_a[0].reshape(na, PLE_DIM), row(norm_ple[0]), wg, wp, row(final_norm), ys, 0)
    y_b = fin(p_b[0].reshape(nb, PLE_DIM), row(norm_ple[0]), wg, wp, row(final_norm), ys, na)
    return y_a.reshape(ba, sa, D_MODEL), y_b.reshape(bb, sb, D_MODEL)


def kernel(x_prompt, x_sample, p_prompt, p_sample, norm_mix, w_in, lambda_q1, lambda_k1, lambda_q2, lambda_k2, subln_g, fourier_norm, w_out, norm_ffn, router_w, router_b, w_up, b_up, w_down, b_down, norm_ple, w_ple_gate, w_ple_proj, final_norm):
    return _forward(x_prompt, x_sample, p_prompt, p_sample, norm_mix, w_in, lambda_q1, lambda_k1,
                    lambda_q2, lambda_k2, subln_g, fourier_norm, w_out, norm_ffn, router_w, router_b,
                    w_up, b_up, w_down, b_down, norm_ple, w_ple_gate, w_ple_proj, final_norm)
```

```python
import functools
import math

import jax
import jax.numpy as jnp
from jax import lax
from jax.experimental import pallas as pl
from jax.experimental.pallas import tpu as pltpu

F32 = jnp.float32
BF16 = jnp.bfloat16
I32 = jnp.int32

D_MODEL = 1024
ATTN_WIDTH = 512
FOURIER_WIDTH = 512
HEAD_DIM = 64
N_HEADS = 4
N_GROUPS = 4
GROUP = 128
N_EXPERTS = 32
TOP_K = 4
D_EXPERT = 1024
SWIGLU_LIMIT = 7.0
SWIGLU_ALPHA = 1.702
PLE_DIM = 256
EPS = 1e-6
LAM_INIT = 0.8 - 0.6 * math.exp(-0.3 * 0)
TWIDDLE_ROWS = 64
ROW_UNROLL = 8

VMEM_LIMIT = 56 * 1024 * 1024


def _params(n_axes):
    return pltpu.CompilerParams(dimension_semantics=("arbitrary",) * n_axes,
                                vmem_limit_bytes=VMEM_LIMIT)


def _rms(x, g):
    return x * lax.rsqrt(jnp.mean(x * x, axis=-1, keepdims=True) + EPS) * g


def _in_proj_kernel(nb_a, xa_ref, xb_ref, g_ref, w_ref, dft_ref, qkv_ref, ab_ref):
    i = pl.program_id(0)
    x = jnp.where(i < nb_a, xa_ref[...], xb_ref[...])
    h = _rms(x, g_ref[...]).astype(BF16)
    for c in range(3):
        pc = jnp.dot(h, w_ref[:, c * ATTN_WIDTH:(c + 1) * ATTN_WIDTH], preferred_element_type=F32)
        if c == 0:
            pc = pc * (HEAD_DIM ** -0.5)
        qkv_ref[:, c * ATTN_WIDTH:(c + 1) * ATTN_WIDTH] = pc.astype(BF16)
    u = jnp.dot(h, w_ref[:, 3 * ATTN_WIDTH:], preferred_element_type=F32).astype(BF16)
    for g in range(N_GROUPS):
        r = jnp.dot(u[:, g * GROUP:(g + 1) * GROUP], dft_ref[...], preferred_element_type=F32)
        ab_ref[:, g * GROUP:(g + 1) * GROUP] = r[:, :GROUP].astype(BF16)
        ab_ref[:, FOURIER_WIDTH + g * GROUP:FOURIER_WIDTH + (g + 1) * GROUP] = r[:, GROUP:].astype(BF16)


def _in_proj(xa, xb, g, w_bf, dft_bf, tm):
    na, nb = xa.shape[0], xb.shape[0]
    nba, nbb = na // tm, nb // tm
    n = na + nb
    return pl.pallas_call(
        functools.partial(_in_proj_kernel, nba),
        grid=(nba + nbb,),
        in_specs=[
            pl.BlockSpec((tm, D_MODEL), lambda i: (jnp.minimum(i, nba - 1), 0)),
            pl.BlockSpec((tm, D_MODEL), lambda i: (jnp.maximum(i - nba, 0), 0)),
            pl.BlockSpec((1, D_MODEL), lambda i: (0, 0)),
            pl.BlockSpec((D_MODEL, 4 * ATTN_WIDTH), lambda i: (0, 0)),
            pl.BlockSpec((GROUP, 2 * GROUP), lambda i: (0, 0)),
        ],
        out_specs=[
            pl.BlockSpec((tm, 3 * ATTN_WIDTH), lambda i: (i, 0)),
            pl.BlockSpec((tm, 2 * FOURIER_WIDTH), lambda i: (i, 0)),
        ],
        out_shape=[jax.ShapeDtypeStruct((n, 3 * ATTN_WIDTH), BF16),
                   jax.ShapeDtypeStruct((n, 2 * FOURIER_WIDTH), BF16)],
        compiler_params=_params(1),
        name="in_proj",
    )(xa, xb, g, w_bf, dft_bf)


def _alibi_lanes(m, lane):
    data = (lane < HEAD_DIM) if m == 0 else (lane >= HEAD_DIM)
    return data, (HEAD_DIM if m == 0 else 0)


def _attn_kernel(seq, tq, lq1_ref, lk1_ref, lq2_ref, lk2_ref, g_ref, q_ref, k_ref, v_ref, o_ref,
                 kaug_sc, dbias_sc, qa_sc, m_sc0, l_sc0, a_sc0, m_sc1, l_sc1, a_sc1):
    h = pl.program_id(1)
    qi = pl.program_id(2)
    nk = seq // tq
    slope = jnp.where(h == 0, 0.25, jnp.where(h == 1, 0.0625, jnp.where(h == 2, 0.015625, 0.00390625)))
    slope = slope.astype(F32)
    m_sc, l_sc, a_sc = (m_sc0, m_sc1), (l_sc0, l_sc1), (a_sc0, a_sc1)
    nt = (((1,), (1,)), ((), ()))
    lane = lax.broadcasted_iota(I32, (tq, 2 * HEAD_DIM), 1)
    rows = lax.broadcasted_iota(I32, (tq, 2 * HEAD_DIM), 0)

    def features(fb, f0, f1, f2, f3):
        return jnp.where(lane == fb, f0, jnp.where(lane == fb + 1, f1, jnp.where(
            lane == fb + 2, f2, jnp.where(lane == fb + 3, f3, 0.0))))

    @pl.when(qi == 0)
    def _():
        r = lax.broadcasted_iota(I32, (tq, tq), 0)
        c = lax.broadcasted_iota(I32, (tq, tq), 1)
        dbias_sc[...] = -slope * jnp.abs(r - c).astype(F32)

        def build(j, carry):
            start = pl.multiple_of(j * tq, tq)
            kc = k_ref[pl.ds(start, tq), :]
            pos = start + rows
            jh = (pos >> 6).astype(F32)
            jl = (pos & 63).astype(F32)
            for m in range(2):
                data, fb = _alibi_lanes(m, lane)
                left = features(fb, -64.0 * slope, -slope, 64.0 * slope * jh, slope * jl)
                kaug_sc[m, 0, pl.ds(start, tq), :] = jnp.where(data, kc, left.astype(BF16))
                kaug_sc[m, 1, pl.ds(start, tq), :] = jnp.where(data, kc, (-left).astype(BF16))
            return carry

        lax.fori_loop(0, nk, build, 0)

    q = q_ref[...]
    ipos = qi * tq + rows
    q_plain = []
    for m in range(2):
        data, fb = _alibi_lanes(m, lane)
        feat = features(fb, (ipos >> 6).astype(F32), (ipos & 63).astype(F32), 1.0, 1.0)
        qa_sc[m] = jnp.where(data, q, feat.astype(BF16))
        q_plain.append(jnp.where(data, q, jnp.zeros_like(q)))

    def softmax_block(s, m_new):
        parts = [jnp.exp(s[:, g * 128:(g + 1) * 128] - m_new) for g in range(tq // 128)]
        row_part = parts[0]
        for p in parts[1:]:
            row_part = row_part + p
        return jnp.concatenate(parts, axis=1).astype(BF16), row_part

    dstart = pl.multiple_of(qi * tq, tq)
    vd = v_ref[pl.ds(dstart, tq), :]
    for m in range(2):
        s = lax.dot_general(q_plain[m], kaug_sc[m, 0, pl.ds(dstart, tq), :], nt,
                            preferred_element_type=F32) + dbias_sc[...]
        m_new = jnp.broadcast_to(jnp.max(s, axis=1, keepdims=True), (tq, 128))
        p, row_part = softmax_block(s, m_new)
        m_sc[m][...] = m_new
        l_sc[m][...] = row_part
        a_sc[m][...] = jnp.dot(p, vd, preferred_element_type=F32)

    def body(jj, carry):
        j = jj + (jj >= qi).astype(I32)
        side = (j > qi).astype(I32)
        start = pl.multiple_of(j * tq, tq)
        vc = v_ref[pl.ds(start, tq), :]
        for m in range(2):
            s = lax.dot_general(qa_sc[m], kaug_sc[m, side, pl.ds(start, tq), :], nt,
                                preferred_element_type=F32)
            m_old = m_sc[m][...]
            m_new = jnp.maximum(m_old, jnp.max(s, axis=1, keepdims=True))
            alpha = jnp.exp(m_old - m_new)
            p, row_part = softmax_block(s, m_new)
            m_sc[m][...] = m_new
            l_sc[m][...] = alpha * l_sc[m][...] + row_part
            a_sc[m][...] = alpha * a_sc[m][...] + jnp.dot(p, vc, preferred_element_type=F32)
        return carry

    lax.fori_loop(0, nk - 1, body, 0, unroll=True)

    lam =(jnp.exp(jnp.sum(lq1_ref[...] * lk1_ref[...], axis=-1, keepdims=True))
           - jnp.exp(jnp.sum(lq2_ref[...] * lk2_ref[...], axis=-1, keepdims=True)) + LAM_INIT)
    o = (a_sc0[...] / jnp.sum(l_sc0[...], axis=1, keepdims=True)
         - lam * (a_sc1[...] / jnp.sum(l_sc1[...], axis=1, keepdims=True)))
    o_ref[...] = (_rms(o, g_ref[...]) * (1.0 - LAM_INIT)).astype(BF16)


def _attention(qkv, row_off, batch, seq, lams, subln_g, tq):
    assert seq <= 64 * 64 and seq % tq == 0 and tq % 128 == 0
    nq = seq // tq
    qb0 = row_off // tq
    kb0 = row_off // seq
    vec = pl.BlockSpec((1, HEAD_DIM), lambda b, h, i: (0, 0))
    stat = pltpu.VMEM((tq, 2 * HEAD_DIM), F32)
    return pl.pallas_call(
        functools.partial(_attn_kernel, seq, tq),
        grid=(batch, N_HEADS, nq),
        in_specs=[vec, vec, vec, vec,
                  pl.BlockSpec((1, 2 * HEAD_DIM), lambda b, h, i: (0, 0)),
                  pl.BlockSpec((tq, 2 * HEAD_DIM), lambda b, h, i: (qb0 + b * nq + i, h)),
                  pl.BlockSpec((seq, 2 * HEAD_DIM), lambda b, h, i: (kb0 + b, N_HEADS + h)),
                  pl.BlockSpec((seq, 2 * HEAD_DIM), lambda b, h, i: (kb0 + b, 2 * N_HEADS + h))],
        out_specs=pl.BlockSpec((tq, 2 * HEAD_DIM), lambda b, h, i: (b * nq + i, h)),
        out_shape=jax.ShapeDtypeStruct((batch * seq, ATTN_WIDTH), BF16),
        scratch_shapes=[pltpu.VMEM((2, 2, seq, 2 * HEAD_DIM), BF16),
                        pltpu.VMEM((tq, tq), F32),
                        pltpu.VMEM((2, tq, 2 * HEAD_DIM), BF16),
                        stat, stat, stat, stat, stat, stat],
        compiler_params=_params(3),
        name="attention",
    )(*lams, subln_g, qkv, qkv, qkv)


def _seq_dft_kernel(seq, tm, t1c_ref, t1s_ref, t2c_ref, t2s_ref, g_ref, ab_ref, f_ref, lc_sc, ls_sc):
    b = pl.program_id(1)

    @pl.when(b == 0)
    def _():
        t2c = t2c_ref[...]
        t2s = t2s_ref[...]
        for a in range(tm // TWIDDLE_ROWS):
            c1 = t1c_ref[a:a + 1, :]
            s1 = t1s_ref[a:a + 1, :]
            rows = slice(a * TWIDDLE_ROWS, (a + 1) * TWIDDLE_ROWS)
            lc_sc[rows, :] = (c1 * t2c - s1 * t2s).astype(BF16)
            ls_sc[rows, :] = (-(s1 * t2c + c1 * t2s)).astype(BF16)

    f = (jnp.dot(lc_sc[...], ab_ref[:, :FOURIER_WIDTH], preferred_element_type=F32)
         + jnp.dot(ls_sc[...], ab_ref[:, FOURIER_WIDTH:], preferred_element_type=F32))
    f = f * (1.0 / math.sqrt(seq * GROUP))
    f_ref[...] = _rms(f, g_ref[...]).astype(BF16)


def _twiddle_tables(seq):
    k = jnp.arange(seq, dtype=I32)[None, :]
    a = (jnp.arange(seq // TWIDDLE_ROWS, dtype=I32) * TWIDDLE_ROWS)[:, None]
    b = jnp.arange(TWIDDLE_ROWS, dtype=I32)[:, None]
    ang1 = ((a * k) % seq).astype(F32) * (2.0 * math.pi / seq)
    ang2 = ((b * k) % seq).astype(F32) * (2.0 * math.pi / seq)
    return jnp.cos(ang1), jnp.sin(ang1), jnp.cos(ang2), jnp.sin(ang2)


def _seq_dft(ab, row_off, batch, seq, fourier_g, tm):
    t1c, t1s, t2c, t2s = _twiddle_tables(seq)
    nr = seq // tm
    na = tm // TWIDDLE_ROWS
    kb0 = row_off // seq
    t1 = pl.BlockSpec((na, seq), lambda i, b: (i, 0))
    t2 = pl.BlockSpec((TWIDDLE_ROWS, seq), lambda i, b: (0, 0))
    return pl.pallas_call(
        functools.partial(_seq_dft_kernel, seq, tm),
        grid=(nr, batch),
        in_specs=[t1, t1, t2, t2,
                  pl.BlockSpec((1, FOURIER_WIDTH), lambda i, b: (0, 0)),
                  pl.BlockSpec((seq, 2 * FOURIER_WIDTH), lambda i, b: (kb0 + b, 0))],
        out_specs=pl.BlockSpec((tm, FOURIER_WIDTH), lambda i, b: (b * nr + i, 0)),
        out_shape=jax.ShapeDtypeStruct((batch * seq, FOURIER_WIDTH), BF16),
        scratch_shapes=[pltpu.VMEM((tm, seq), BF16), pltpu.VMEM((tm, seq), BF16)],
        compiler_params=_params(2),
        name="seq_dft",
    )(t1c, t1s, t2c, t2s, fourier_g, ab)


def _out_route_kernel(nb_a, tm, xa_ref, xb_ref, oa_ref, ob_ref, fa_ref, fb_ref, wo_ref, g_ref,
                      rw_ref, rb_ref, x1_ref, h2_ref, e_ref, r_ref, gt_ref, cnt_ref, carry_sc):
    i = pl.program_id(0)
    first = i < nb_a

    @pl.when(i == 0)
    def _():
        carry_sc[...] = jnp.zeros(carry_sc.shape, F32)

    x = jnp.where(first, xa_ref[...], xb_ref[...])
    o = jnp.where(first, oa_ref[...], ob_ref[...])
    f = jnp.where(first, fa_ref[...], fb_ref[...])
    x1 = (x + jnp.dot(o, wo_ref[:ATTN_WIDTH, :], preferred_element_type=F32)
          + jnp.dot(f, wo_ref[ATTN_WIDTH:, :], preferred_element_type=F32))
    x1_ref[...] = x1
    h2 = _rms(x1, g_ref[...])
    h2_ref[...] = h2

    lg = lax.dot_general(rw_ref[...], h2.astype(BF16), (((1,), (1,)), ((), ())),
                         preferred_element_type=F32) + rb_ref[...]
    eidx = lax.broadcasted_iota(I32, lg.shape, 0)
    work = lg
    vals, hots = [], []
    for k in range(TOP_K):
        mx = jnp.max(work, axis=0, keepdims=True)
        sel = jnp.min(jnp.where(work == mx, eidx, N_EXPERTS), axis=0, keepdims=True)
        hot = eidx == sel
        work = jnp.where(hot, -jnp.inf, work)
        vals.append(mx)
        hots.append(hot)
        e_ref[0, k:k + 1, :] = sel
    ex = [jnp.exp(v - vals[0]) for v in vals]
    den = ex[0] + ex[1] + ex[2] + ex[3]
    for k in range(TOP_K):
        gt_ref[0, k:k + 1, :] = ex[k] / den

    multi = jnp.zeros(lg.shape, F32)
    for k in range(TOP_K):
        multi = multi + hots[k].astype(F32)
    tri = (lax.broadcasted_iota(I32, (tm, tm), 0) < lax.broadcasted_iota(I32, (tm, tm), 1)).astype(BF16)
    base = jnp.dot(multi.astype(BF16), tri, preferred_element_type=F32) + carry_sc[...]
    for k in range(TOP_K):
        r_ref[0, k:k + 1, :] = jnp.sum(jnp.where(hots[k], base, 0.0), axis=0, keepdims=True).astype(I32)
    carry_sc[...] = carry_sc[...] + jnp.sum(multi, axis=1, keepdims=True)
    cnt_ref[...] = carry_sc[...].astype(I32)


def _out_route(xa, xb, oa, ob, fa, fb, wo_bf, g, rw_t_bf, rb_col, tm):
    na, nb = xa.shape[0], xb.shape[0]
    nba, nbb = na // tm, nb // tm
    n, nblk = na + nb, nba + nbb

    def sel_a(w):
        return pl.BlockSpec((tm, w), lambda i: (jnp.minimum(i, nba - 1), 0))

    def sel_b(w):
        return pl.BlockSpec((tm, w), lambda i: (jnp.maximum(i - nba, 0), 0))

    slot = pl.BlockSpec((1, TOP_K, tm), lambda i: (i, 0, 0))
    return pl.pallas_call(
        functools.partial(_out_route_kernel, nba, tm),
        grid=(nblk,),
        in_specs=[sel_a(D_MODEL), sel_b(D_MODEL), sel_a(ATTN_WIDTH), sel_b(ATTN_WIDTH),
                  sel_a(FOURIER_WIDTH), sel_b(FOURIER_WIDTH),
                  pl.BlockSpec((D_MODEL, D_MODEL), lambda i: (0, 0)),
                  pl.BlockSpec((1, D_MODEL), lambda i: (0, 0)),
                  pl.BlockSpec((N_EXPERTS, D_MODEL), lambda i: (0, 0)),
                  pl.BlockSpec((N_EXPERTS, 1), lambda i: (0, 0))],
        out_specs=[pl.BlockSpec((tm, D_MODEL), lambda i: (i, 0)),
                   pl.BlockSpec((tm, D_MODEL), lambda i: (i, 0)),
                   slot, slot, slot,
                   pl.BlockSpec((N_EXPERTS, 1), lambda i: (0, 0))],
        out_shape=[jax.ShapeDtypeStruct((n, D_MODEL), F32),
                   jax.ShapeDtypeStruct((n, D_MODEL), F32),
                   jax.ShapeDtypeStruct((nblk, TOP_K, tm), I32),
                   jax.ShapeDtypeStruct((nblk, TOP_K, tm), I32),
                   jax.ShapeDtypeStruct((nblk, TOP_K, tm), F32),
                   jax.ShapeDtypeStruct((N_EXPERTS, 1), I32)],
        scratch_shapes=[pltpu.VMEM((N_EXPERTS, 1), F32)],
        compiler_params=_params(1),
        name="out_route",
    )(xa, xb, oa, ob, fa, fb, wo_bf, g, rw_t_bf, rb_col)


def _dest_kernel(ps_ref, e_ref, r_ref, d_ref):
    e = e_ref[...]
    acc = jnp.zeros(e.shape, I32)
    for x in range(N_EXPERTS):
        acc = jnp.where(e == x, ps_ref[x], acc)
    d = acc + r_ref[...]
    tm = d.shape[2]
    for k in range(TOP_K):
        d_ref[0, :, k * tm:(k + 1) * tm] = d[0, k:k + 1, :]


def _dest(pad_starts, top_e, rank):
    nblk, _, tm = top_e.shape
    slot = pl.BlockSpec((1, TOP_K, tm), lambda i: (i, 0, 0))
    return pl.pallas_call(
        _dest_kernel,
        grid=(nblk,),
        in_specs=[pl.BlockSpec(memory_space=pltpu.SMEM), slot, slot],
        out_specs=pl.BlockSpec((1, 1, TOP_K * tm), lambda i: (i, 0, 0)),
        out_shape=jax.ShapeDtypeStruct((nblk, 1, TOP_K * tm), I32),
        compiler_params=_params(1),
        name="dest",
    )(pad_starts, top_e, rank)


def _dispatch_kernel(tm, tme, n_zero, zb_ref, d_ref, h_ref, xs_ref, zeros_sc, sem, zsem):
    i = pl.program_id(0)

    @pl.when(i == 0)
    def _():
        zeros_sc[...] = jnp.zeros(zeros_sc.shape, F32)

        def zero_copy(j):
            return pltpu.make_async_copy(zeros_sc, xs_ref.at[pl.ds(zb_ref[j] * tme, tme)], zsem)

        for j in range(n_zero):
            pl.when(zb_ref[j] >= 0)(lambda j=j: zero_copy(j).start())
        for j in range(n_zero):
            pl.when(zb_ref[j] >= 0)(lambda j=j: zero_copy(j).wait())

    def body(t8, carry):
        base = pl.multiple_of(t8 * ROW_UNROLL, ROW_UNROLL)
        for u in range(ROW_UNROLL):
            for k in range(TOP_K):
                pltpu.make_async_copy(h_ref.at[pl.ds(base + u, 1)],
                                      xs_ref.at[pl.ds(d_ref[0, 0, k * tm + base + u], 1)],
                                      sem).start(priority=k % 2)
        return carry

    lax.fori_loop(0, tm // ROW_UNROLL, body, 0)
    for k in range(TOP_K):
        pltpu.make_async_copy(h_ref, xs_ref.at[pl.ds(0, tm)], sem).wait()


def _dispatch(zero_blocks, dest, h2, n_rows, tme):
    nblk, tm = dest.shape[0], dest.shape[2] // TOP_K
    n_zero = zero_blocks.shape[0]
    return pl.pallas_call(
        functools.partial(_dispatch_kernel, tm, tme, n_zero),
        grid_spec=pltpu.PrefetchScalarGridSpec(
            num_scalar_prefetch=1,
            grid=(nblk,),
            in_specs=[pl.BlockSpec((1, 1, TOP_K * tm), lambda i, zb: (i, 0, 0), memory_space=pltpu.SMEM),
                      pl.BlockSpec((tm, D_MODEL), lambda i, zb: (i, 0))],
            out_specs=pl.BlockSpec(memory_space=pl.ANY),
            scratch_shapes=[pltpu.VMEM((tme, D_MODEL), F32), pltpu.SemaphoreType.DMA(()),
                            pltpu.SemaphoreType.DMA(())]),
        out_shape=jax.ShapeDtypeStruct((n_rows, D_MODEL), F32),
        compiler_params=_params(1),
        name="dispatch",
    )(zero_blocks, dest, h2)


def _experts_kernel(be_ref, nu_ref, xs_ref, wu_ref, bu_ref, wd_ref, bd_ref, ys_ref, a_sc, wu_sc, wd_sc):
    i = pl.program_id(0)
    used = i < nu_ref[0]
    new_expert = (i == 0) | (be_ref[i] != be_ref[jnp.maximum(i - 1, 0)])

    @pl.when(used & new_expert)
    def _():
        rows = 128
        for r in range(D_MODEL // rows):
            wu_sc[r * rows:(r + 1) * rows, :] = wu_ref[0, r * rows:(r + 1) * rows, :].astype(BF16)
        for r in range(D_EXPERT // rows):
            wd_sc[r * rows:(r + 1) * rows, :] = wd_ref[0, r * rows:(r + 1) * rows, :].astype(BF16)

    @pl.when(used)
    def _():
        x = xs_ref[...].astype(BF16)
        cw = 256
        for c in range(D_EXPERT // cw):
            glu = jnp.dot(x, wu_sc[:, c * cw:(c + 1) * cw], preferred_element_type=F32) \
                + bu_ref[0, :, c * cw:(c + 1) * cw]
            lin = jnp.dot(x, wu_sc[:, D_EXPERT + c * cw:D_EXPERT + (c + 1) * cw],
                          preferred_element_type=F32) + bu_ref[0, :, D_EXPERT + c * cw:D_EXPERT + (c + 1) * cw]
            glu = jnp.minimum(glu, SWIGLU_LIMIT)
            lin = jnp.clip(lin, -SWIGLU_LIMIT, SWIGLU_LIMIT)
            a = glu * jax.nn.sigmoid(SWIGLU_ALPHA * glu) * (lin + 1.0)
            a_sc[:, c * cw:(c + 1) * cw] = a.astype(BF16)
        ys_ref[...] = jnp.dot(a_sc[...], wd_sc[...], preferred_element_type=F32) + bd_ref[0]

    @pl.when(i >= nu_ref[0])
    def _():
        ys_ref[...] = jnp.zeros(ys_ref.shape, F32)


def _experts(block_e, n_used, xs, wu_bf, bu, wd_bf, bd, tme):
    n_rows = xs.shape[0]
    nb = n_rows // tme

    def row_map(i, be, nu):
        return (jnp.minimum(i, nu[0] - 1), 0)

    def out_map(i, be, nu):
        return (i, 0)

    def exp_map(i, be, nu):
        return (be[i], 0, 0)

    return pl.pallas_call(
        _experts_kernel,
        grid_spec=pltpu.PrefetchScalarGridSpec(
            num_scalar_prefetch=2,
            grid=(nb,),
            in_specs=[pl.BlockSpec((tme, D_MODEL), row_map),
                      pl.BlockSpec((1, D_MODEL, 2 * D_EXPERT), exp_map),
                      pl.BlockSpec((1, 1, 2 * D_EXPERT), exp_map),
                      pl.BlockSpec((1, D_EXPERT, D_MODEL), exp_map),
                      pl.BlockSpec((1, 1, D_MODEL), exp_map)],
            out_specs=pl.BlockSpec((tme, D_MODEL), out_map),
            scratch_shapes=[pltpu.VMEM((tme, D_EXPERT), BF16),
                            pltpu.VMEM((D_MODEL, 2 * D_EXPERT), BF16),
                            pltpu.VMEM((D_EXPERT, D_MODEL), BF16)]),
        out_shape=jax.ShapeDtypeStruct((n_rows, D_MODEL), F32),
        compiler_params=_params(1),
        name="experts",
    )(block_e, n_used, xs, wu_bf, bu, wd_bf, bd)


def _final_kernel(tm, d_ref, x1_ref, gt_ref, p_ref, gp_ref, wg_ref, wp_ref, gf_ref, ys_ref, y_ref,
                  buf, sem):
    def body(t8, carry):
        base = pl.multiple_of(t8 * ROW_UNROLL, ROW_UNROLL)
        for u in range(ROW_UNROLL):
            for k in range(TOP_K):
                pltpu.make_async_copy(ys_ref.at[pl.ds(d_ref[0, 0, k * tm + base + u], 1)],
                                      buf.at[k, pl.ds(base + u, 1)], sem).start(priority=k % 2)
        return carry

    lax.fori_loop(0, tm // ROW_UNROLL, body, 0)
    for k in range(TOP_K):
        pltpu.make_async_copy(ys_ref.at[pl.ds(0, tm)], buf.at[k], sem).wait()

    x2 = x1_ref[...]
    for k in range(TOP_K):
        x2 = x2 + buf[k] * gt_ref[:, k:k + 1]
    gate = jax.nn.sigmoid(jnp.dot(_rms(x2, gp_ref[...]).astype(BF16), wg_ref[...],
                                  preferred_element_type=F32))
    x3 = x2 + gate * jnp.dot(p_ref[...].astype(BF16), wp_ref[...], preferred_element_type=F32)
    y_ref[...] = _rms(x3, gf_ref[...])


def _final(dest, x1, gates_tok, p, g_ple, wg_bf, wp_bf, g_fin, ys, row_off):
    tm = dest.shape[2] // TOP_K
    n_tok = p.shape[0]
    b0 = row_off // tm
    return pl.pallas_call(
        functools.partial(_final_kernel, tm),
        grid=(n_tok // tm,),
        in_specs=[pl.BlockSpec((1, 1, TOP_K * tm), lambda i: (b0 + i, 0, 0), memory_space=pltpu.SMEM),
                  pl.BlockSpec((tm, D_MODEL), lambda i: (b0 + i, 0)),
                  pl.BlockSpec((tm, TOP_K), lambda i: (b0 + i, 0)),
                  pl.BlockSpec((tm, PLE_DIM), lambda i: (i, 0)),
                  pl.BlockSpec((1, D_MODEL), lambda i: (0, 0)),
                  pl.BlockSpec((D_MODEL, D_MODEL), lambda i: (0, 0)),
                  pl.BlockSpec((PLE_DIM, D_MODEL), lambda i: (0, 0)),
                  pl.BlockSpec((1, D_MODEL), lambda i: (0, 0)),
                  pl.BlockSpec(memory_space=pl.ANY)],
        out_specs=pl.BlockSpec((tm, D_MODEL), lambda i: (i, 0)),
        out_shape=jax.ShapeDtypeStruct((n_tok, D_MODEL), F32),
        scratch_shapes=[pltpu.VMEM((TOP_K, tm, D_MODEL), F32), pltpu.SemaphoreType.DMA(())],
        compiler_params=_params(1),
        name="final",
    )(dest, x1, gates_tok, p, g_ple, wg_bf, wp_bf, g_fin, ys)


def _tiles(seq_a, seq_b):
    smin = min(seq_a, seq_b)
    return dict(tm=min(512, smin), tq=min(512, smin), tk=min(512, smin), tf=min(512, smin),
                tr=min(512, smin), tme=min(512, smin))


def _forward(x_a, x_b, p_a, p_b, norm_mix, w_in, lambda_q1, lambda_k1, lambda_q2, lambda_k2, subln_g,
             fourier_norm, w_out, norm_ffn, router_w, router_b, w_up, b_up, w_down, b_down, norm_ple,
             w_ple_gate, w_ple_proj, final_norm):
    (ba, sa, _), (bb, sb, _) = x_a.shape, x_b.shape
    na, nb = ba * sa, bb * sb
    n = na + nb
    t = _tiles(sa, sb)
    xa2, xb2 = x_a.reshape(na, D_MODEL), x_b.reshape(nb, D_MODEL)

    c = jnp.arange(GROUP, dtype=I32)
    ang = ((c[:, None] * c[None, :]) % GROUP).astype(F32) * (2.0 * math.pi / GROUP)
    dft = jnp.concatenate([jnp.cos(ang), jnp.sin(ang)], axis=1).astype(BF16)
    row = lambda v: v.reshape(1, -1).astype(F32)

    qkv, ab = _in_proj(xa2, xb2, row(norm_mix[0]), w_in[0].astype(BF16), dft, t["tm"])

    lams = (row(lambda_q1[0]), row(lambda_k1[0]), row(lambda_q2[0]), row(lambda_k2[0]))
    o_a = _attention(qkv, 0, ba, sa, lams, row(subln_g[0]), t["tq"])
    o_b = _attention(qkv, na, bb, sb, lams, row(subln_g[0]), t["tq"])
    f_a = _seq_dft(ab, 0, ba, sa, row(fourier_norm[0]), t["tf"])
    f_b = _seq_dft(ab, na, bb, sb, row(fourier_norm[0]), t["tf"])

    tr = t["tr"]
    x1, h2, top_e, rank, gates, counts = _out_route(
        xa2, xb2, o_a, o_b, f_a, f_b, w_out[0].astype(BF16), row(norm_ffn[0]),
        router_w[0].T.astype(BF16), router_b[0].reshape(N_EXPERTS, 1).astype(F32), tr)

    tme = t["tme"]
    counts = counts.reshape(N_EXPERTS)
    blocks = (counts + tme - 1) // tme
    blk_ends = jnp.cumsum(blocks)
    pad_starts = ((blk_ends - blocks) * tme).astype(I32)
    n_blocks = (n * TOP_K) // tme + N_EXPERTS
    n_used = blk_ends[-1:].astype(I32)
    bidx = jnp.minimum(jnp.arange(n_blocks, dtype=I32), n_used[0] - 1)
    block_e = jnp.sum((bidx[:, None] >= blk_ends[None, :]).astype(I32), axis=1).astype(I32)

    tail = n_used[0] + jnp.arange(N_EXPERTS, dtype=I32)
    zero_blocks = jnp.concatenate([jnp.where(blocks > 0, blk_ends - 1, -1).astype(I32),
                                   jnp.where(tail < n_blocks, tail, -1)])

    dest = _dest(pad_starts, top_e, rank)
    xs = _dispatch(zero_blocks, dest, h2, n_blocks * tme, tme)
    ys = _experts(block_e, n_used, xs, w_up[0], b_up[0].reshape(N_EXPERTS, 1, -1),
                  w_down[0], b_down[0].reshape(N_EXPERTS, 1, -1), tme)

    gates_tok = jnp.transpose(gates, (0, 2, 1)).reshape(n, TOP_K)
    fin = functools.partial(_final, dest, x1, gates_tok)
    wg, wp = w_ple_gate[0].astype(BF16), w_ple_proj[0].astype(BF16)
    y_a = fin(p_a[0].reshape(na, PLE_DIM), row(norm_ple[0]), wg, wp, row(final_norm), ys, 0)
    y_b = fin(p_b[0].reshape(nb, PLE_DIM), row(norm_ple[0]), wg, wp, row(final_norm), ys, na)
    return y_a.reshape(ba, sa, D_MODEL), y_b.reshape(bb, sb, D_MODEL)


def kernel(x_prompt, x_sample, p_prompt, p_sample, norm_mix, w_in, lambda_q1, lambda_k1, lambda_q2, lambda_k2, subln_g, fourier_norm, w_out, norm_ffn, router_w, router_b, w_up, b_up, w_down, b_down, norm_ple, w_ple_gate, w_ple_proj, final_norm):
    return _forward(x_prompt, x_sample, p_prompt, p_sample, norm_mix, w_in, lambda_q1, lambda_k1,
                    lambda_q2, lambda_k2, subln_g, fourier_norm, w_out, norm_ffn, router_w, router_b,
                    w_up, b_up, w_down, b_down, norm_ple, w_ple_gate, w_ple_proj, final_norm)
```
